```python
import math
import jax, jax.numpy as jnp
from jax import lax
import numpy as np

D_MODEL = 4096
BATCH = 4
SEQ = 2048
DEPTH = 2
DEC_BATCH = 8
DEC_SEQ = 1
PAST_LEN = 16384
PAGE_SIZE = 128

N_CONV_LAYERS = (DEPTH + 1) // 2
N_ATTN_LAYERS = DEPTH // 2
D_A = D_MODEL // 2
CONV_A_W = 3
D_B = D_MODEL // 2
CONV_B_W = 31
HD = 128
H_SB = (D_MODEL // 2) // HD
D_SB = H_SB * HD
H_DF = (D_MODEL // 2) // (2 * HD)
D_DF = H_DF * 2 * HD
D_CONV_IN = 3 * D_A + 2 * D_B
D_ATTN_IN = 3 * D_SB + 3 * D_DF
D_FF = ((8 * D_MODEL // 3 + 255) // 256) * 256
FFN_CONV_W = 3
ROPE_THETA = 10000.0
Q_BLOCK = 128
LN_EPS = 1e-5
RMS_EPS = 1e-5
DEEPNORM_ALPHA = (2 * DEPTH) ** 0.25
DEEPNORM_BETA = (8 * DEPTH) ** -0.25

kernel_name = 'hybrid_conv_stickbreak_diff_decoder'


def _proj(x, w):
    return jnp.einsum('btd,de->bte', x, w)


def layer_norm(x, g, b):
    xf = x.astype(jnp.float32)
    mu = jnp.mean(xf, axis=-1, keepdims=True)
    var = jnp.mean(jnp.square(xf - mu), axis=-1, keepdims=True)
    y = (xf - mu) * lax.rsqrt(var + LN_EPS)
    return (y * g.astype(jnp.float32) + b.astype(jnp.float32)).astype(x.dtype)


def rms_norm(x, g):
    xf = x.astype(jnp.float32)
    y = xf * lax.rsqrt(jnp.mean(jnp.square(xf), axis=-1, keepdims=True) + RMS_EPS)
    return (y * g.astype(jnp.float32)).astype(x.dtype)


def causal_dwconv(x_ext, w):
    return lax.conv_general_dilated(
        x_ext, w[:, None, :].astype(x_ext.dtype), window_strides=(1,), padding='VALID',
        dimension_numbers=('NWC', 'WIO', 'NWC'), feature_group_count=x_ext.shape[-1])


def rope(x, pos):
    half = x.shape[-1] // 2
    inv_freq = ROPE_THETA ** (-jnp.arange(half, dtype=jnp.float32) / half)
    ang = pos.astype(jnp.float32)[:, None] * inv_freq[None, :]
    bshape = (1, x.shape[1]) + (1,) * (x.ndim - 3) + (half,)
    cos = jnp.cos(ang).reshape(bshape)
    sin = jnp.sin(ang).reshape(bshape)
    xf = x.astype(jnp.float32)
    x1, x2 = xf[..., :half], xf[..., half:]
    return jnp.concatenate([x1 * cos - x2 * sin, x2 * cos + x1 * sin], axis=-1).astype(x.dtype)


def sweep_query_blocks(block_fn, q, pos0):
    b, tq = q.shape[0], q.shape[1]
    qb = min(Q_BLOCK, tq)
    n_blk = -(-tq // qb)
    pad = n_blk * qb - tq
    q = jnp.pad(q, [(0, 0), (0, pad)] + [(0, 0)] * (q.ndim - 2))
    q_blocks = jnp.moveaxis(q.reshape((b, n_blk, qb) + q.shape[2:]), 1, 0)
    pos_blocks = (pos0 + jnp.arange(n_blk * qb, dtype=jnp.int32)).reshape(n_blk, qb)
    out = lax.map(lambda args: block_fn(args[0], args[1]), (q_blocks, pos_blocks))
    out = jnp.moveaxis(out, 0, 1)
    return out.reshape((b, n_blk * qb) + out.shape[3:])[:, :tq]


def stick_breaking_attention(q, k, v, pos0):
    kpos = jnp.arange(k.shape[1], dtype=jnp.int32)
    scale = HD ** -0.5

    def block(qb, qpos):
        z = jnp.einsum('bqhd,bkhd->bhqk', qb, k).astype(jnp.float32) * scale
        visible = kpos[None, :] < qpos[:, None]
        log_keep = jnp.where(visible, jax.nn.log_sigmoid(-z), 0.0)
        later = lax.cumsum(log_keep, axis=3, reverse=True) - log_keep
        a = jnp.where(visible, jnp.exp(jax.nn.log_sigmoid(z) + later), 0.0)
        return jnp.einsum('bhqk,bkhd->bqhd', a.astype(v.dtype), v)

    return sweep_query_blocks(block, q, pos0)


def diff_attention(q, k, v, pos0, lam):
    kpos = jnp.arange(k.shape[1], dtype=jnp.int32)
    scale = HD ** -0.5

    def block(qb, qpos):
        s = jnp.einsum('bqhcd,bkhcd->bhcqk', qb, k).astype(jnp.float32) * scale
        visible = kpos[None, :] <= qpos[:, None]
        p = jax.nn.softmax(jnp.where(visible, s, -jnp.inf), axis=-1)
        a = p[:, :, 0] - lam * p[:, :, 1]
        return jnp.einsum('bhqk,bkhe->bqhe', a.astype(v.dtype), v)

    return sweep_query_blocks(block, q, pos0)


def conv_mixer(x, ctx_a, ctx_b, w_in, conv_a_w, conv_b_w, conv_b_b, norm_g, norm_b, w_out):
    proj = _proj(x, w_in)
    gate_b, gate_c, h, glu_a, glu_g = jnp.split(
        proj, [D_A, 2 * D_A, 3 * D_A, 3 * D_A + D_B], axis=-1)
    ext_a = jnp.concatenate([ctx_a, gate_c * h], axis=1)
    y_a = gate_b * causal_dwconv(ext_a, conv_a_w)
    ext_b = jnp.concatenate([ctx_b, glu_a * jax.nn.sigmoid(glu_g)], axis=1)
    y_b = jax.nn.silu(layer_norm(causal_dwconv(ext_b, conv_b_w) + conv_b_b, norm_g, norm_b))
    out = _proj(jnp.concatenate([y_a, y_b], axis=-1), w_out)
    return out, ext_a[:, -(CONV_A_W - 1):], ext_b[:, -(CONV_B_W - 1):]


def attn_mixer(x, pos0, past, w_in, lq1, lk1, lq2, lk2, subln_g, w_out, lambda_init):
    b, t, _ = x.shape
    proj = _proj(x, w_in)
    q_sb, k_sb, v_sb, q_df, k_df, v_df = jnp.split(
        proj, [D_SB, 2 * D_SB, 3 * D_SB, 3 * D_SB + D_DF, 3 * D_SB + 2 * D_DF], axis=-1)
    pos = pos0 + jnp.arange(t, dtype=jnp.int32)
    q_sb = q_sb.reshape(b, t, H_SB, HD)
    k_sb = k_sb.reshape(b, t, H_SB, HD)
    v_sb = v_sb.reshape(b, t, H_SB, HD)
    q_df = rope(q_df.reshape(b, t, H_DF, 2, HD), pos)
    k_df = rope(k_df.reshape(b, t, H_DF, 2, HD), pos)
    v_df = v_df.reshape(b, t, H_DF, 2 * HD)
    new_rows = (k_sb, v_sb, k_df, v_df)
    if past is None:
        ksb_all, vsb_all, kdf_all, vdf_all = new_rows
    else:
        ksb_all, vsb_all, kdf_all, vdf_all = [
            jnp.concatenate([p, n], axis=1) for p, n in zip(past, new_rows)]
    o_sb = stick_breaking_attention(q_sb, ksb_all, vsb_all, pos0)
    f32 = jnp.float32
    lam = (jnp.exp(jnp.sum(lq1.astype(f32) * lk1.astype(f32)))
           - jnp.exp(jnp.sum(lq2.astype(f32) * lk2.astype(f32))) + lambda_init)
    o_df = diff_attention(q_df, kdf_all, vdf_all, pos0, lam)
    o_df = rms_norm(o_df, subln_g) * (1.0 - lambda_init)
    mixed = jnp.concatenate([o_sb.reshape(b, t, D_SB), o_df.reshape(b, t, D_DF)], axis=-1)
    return _proj(mixed, w_out), new_rows


def conv_ffn(x, ctx, w_up, conv_w, conv_b, w_down):
    g, v = jnp.split(_proj(x, w_up), [D_FF], axis=-1)
    ext = jnp.concatenate([ctx, g], axis=1)
    h = jax.nn.gelu(causal_dwconv(ext, conv_w) + conv_b, approximate=False) * v
    return _proj(h, w_down), ext[:, -(FFN_CONV_W - 1):]


def gather_pages(cache, page_table, li):
    rows = cache[page_table, li]
    return rows.reshape((rows.shape[0], rows.shape[1] * rows.shape[2]) + rows.shape[3:])


def setup_inputs(seed: int = 0) -> dict:
    key = jax.random.key(seed)
    keys = iter(jax.random.split(key, 40))

    def normal(shape, scale):
        return jax.random.normal(next(keys), shape, jnp.float32) * scale

    n_pages = PAST_LEN // PAGE_SIZE
    n_used = DEC_BATCH * n_pages
    n_pool = n_used + max(1, n_used // 4)
    page_table = jax.random.permutation(next(keys), n_pool)[:n_used].reshape(DEC_BATCH, n_pages).astype(jnp.int32)
    return {
        'x_prompt': normal((BATCH, SEQ, D_MODEL), 1.0),
        'x_sample': normal((DEC_BATCH, DEC_SEQ, D_MODEL), 1.0),
        'state_conv_a': normal((N_CONV_LAYERS, DEC_BATCH, CONV_A_W - 1, D_A), 1.0),
        'state_conv_b': normal((N_CONV_LAYERS, DEC_BATCH, CONV_B_W - 1, D_B), 1.0),
        'state_ffn': normal((DEPTH, DEC_BATCH, FFN_CONV_W - 1, D_FF), 1.0),
        'cache_sb_k': normal((n_pool, N_ATTN_LAYERS, PAGE_SIZE, H_SB, HD), 1.0),
        'cache_sb_v': normal((n_pool, N_ATTN_LAYERS, PAGE_SIZE, H_SB, HD), 1.0),
        'cache_df_k': normal((n_pool, N_ATTN_LAYERS, PAGE_SIZE, H_DF, 2, HD), 1.0),
        'cache_df_v': normal((n_pool, N_ATTN_LAYERS, PAGE_SIZE, H_DF, 2 * HD), 1.0),
        'page_table': page_table,
        'w_in_conv': normal((N_CONV_LAYERS, D_MODEL, D_CONV_IN), D_MODEL ** -0.5),
        'conv_a_w': normal((N_CONV_LAYERS, CONV_A_W, D_A), CONV_A_W ** -0.5),
        'conv_b_w': normal((N_CONV_LAYERS, CONV_B_W, D_B), CONV_B_W ** -0.5),
        'conv_b_b': normal((N_CONV_LAYERS, D_B), 0.01),
        'norm_b_g': 1.0 + normal((N_CONV_LAYERS, D_B), 0.01),
        'norm_b_b': normal((N_CONV_LAYERS, D_B), 0.01),
        'w_out_conv': normal((N_CONV_LAYERS, D_A + D_B, D_MODEL), (D_A + D_B) ** -0.5 * DEEPNORM_BETA),
        'w_in_attn': normal((N_ATTN_LAYERS, D_MODEL, D_ATTN_IN), D_MODEL ** -0.5),
        'lambda_q1': normal((N_ATTN_LAYERS, HD), 0.1),
        'lambda_k1': normal((N_ATTN_LAYERS, HD), 0.1),
        'lambda_q2': normal((N_ATTN_LAYERS, HD), 0.1),
        'lambda_k2': normal((N_ATTN_LAYERS, HD), 0.1),
        'subln_g': 1.0 + normal((N_ATTN_LAYERS, 2 * HD), 0.01),
        'w_out_attn': normal((N_ATTN_LAYERS, D_SB + D_DF, D_MODEL), (D_SB + D_DF) ** -0.5 * DEEPNORM_BETA),
        'ln1_g': 1.0 + normal((DEPTH, D_MODEL), 0.01),
        'ln1_b': normal((DEPTH, D_MODEL), 0.01),
        'ln2_g': 1.0 + normal((DEPTH, D_MODEL), 0.01),
        'ln2_b': normal((DEPTH, D_MODEL), 0.01),
        'w_ffn_up': normal((DEPTH, D_MODEL, 2 * D_FF), D_MODEL ** -0.5),
        'ffn_conv_w': normal((DEPTH, FFN_CONV_W, D_FF), FFN_CONV_W ** -0.5),
        'ffn_conv_b': normal((DEPTH, D_FF), 0.01),
        'w_ffn_down': normal((DEPTH, D_FF, D_MODEL), D_FF ** -0.5 * DEEPNORM_BETA),
    }


def reference(x_prompt, x_sample, state_conv_a, state_conv_b, state_ffn, cache_sb_k, cache_sb_v,
              cache_df_k, cache_df_v, page_table, w_in_conv, conv_a_w, conv_b_w, conv_b_b,
              norm_b_g, norm_b_b, w_out_conv, w_in_attn, lambda_q1, lambda_k1, lambda_q2, lambda_k2,
              subln_g, w_out_attn, ln1_g, ln1_b, ln2_g, ln2_b, w_ffn_up, ffn_conv_w, ffn_conv_b,
              w_ffn_down):
    past_len = page_table.shape[1] * cache_sb_k.shape[2]
    hp, hs = x_prompt, x_sample
    bp = hp.shape[0]
    ca_p, ca_s, cb_p, cb_s, f_p, f_s = [], [], [], [], [], []
    rows_p = ([], [], [], [])
    rows_s = ([], [], [], [])
    for layer in range(DEPTH):
        i = layer // 2
        if layer % 2 == 0:
            wts = (w_in_conv[i], conv_a_w[i], conv_b_w[i], conv_b_b[i], norm_b_g[i], norm_b_b[i], w_out_conv[i])
            zero_a = jnp.zeros((bp, CONV_A_W - 1, D_A), hp.dtype)
            zero_b = jnp.zeros((bp, CONV_B_W - 1, D_B), hp.dtype)
            mp, sa, sb = conv_mixer(hp, zero_a, zero_b, *wts)
            ca_p.append(sa)
            cb_p.append(sb)
            ms, sa, sb = conv_mixer(hs, state_conv_a[i], state_conv_b[i], *wts)
            ca_s.append(sa)
            cb_s.append(sb)
        else:
            lambda_init = 0.8 - 0.6 * math.exp(-0.3 * layer)
            wts = (w_in_attn[i], lambda_q1[i], lambda_k1[i], lambda_q2[i], lambda_k2[i], subln_g[i], w_out_attn[i], lambda_init)
            past = (gather_pages(cache_sb_k, page_table, i), gather_pages(cache_sb_v, page_table, i),
                    gather_pages(cache_df_k, page_table, i), gather_pages(cache_df_v, page_table, i))
            mp, new_p = attn_mixer(hp, 0, None, *wts)
            ms, new_s = attn_mixer(hs, past_len, past, *wts)
            for lst, r in zip(rows_p, new_p):
                lst.append(r)
            for lst, r in zip(rows_s, new_s):
                lst.append(r)
        hp = layer_norm(DEEPNORM_ALPHA * hp + mp, ln1_g[layer], ln1_b[layer])
        hs = layer_norm(DEEPNORM_ALPHA * hs + ms, ln1_g[layer], ln1_b[layer])
        fw = (w_ffn_up[layer], ffn_conv_w[layer], ffn_conv_b[layer], w_ffn_down[layer])
        fp, sp = conv_ffn(hp, jnp.zeros((bp, FFN_CONV_W - 1, D_FF), hp.dtype), *fw)
        fs, ss = conv_ffn(hs, state_ffn[layer], *fw)
        f_p.append(sp)
        f_s.append(ss)
        hp = layer_norm(DEEPNORM_ALPHA * hp + fp, ln2_g[layer], ln2_b[layer])
        hs = layer_norm(DEEPNORM_ALPHA * hs + fs, ln2_g[layer], ln2_b[layer])
    return (hp, hs,
            jnp.stack(ca_p), jnp.stack(ca_s), jnp.stack(cb_p), jnp.stack(cb_s),
            jnp.stack(f_p), jnp.stack(f_s),
            jnp.stack(rows_p[0], axis=1), jnp.stack(rows_s[0], axis=1),
            jnp.stack(rows_p[1], axis=1), jnp.stack(rows_s[1], axis=1),
            jnp.stack(rows_p[2], axis=1), jnp.stack(rows_s[2], axis=1),
            jnp.stack(rows_p[3], axis=1), jnp.stack(rows_s[3], axis=1))
```

```python
import functools
import math

import jax
import jax.numpy as jnp
from jax import lax
from jax.experimental import pallas as pl
from jax.experimental.pallas import tpu as pltpu

F32 = jnp.float32
BF16 = jnp.bfloat16

HD = 128
CONV_A_W = 3
CONV_B_W = 31
FFN_CONV_W = 3
ROPE_THETA = 10000.0
LN_EPS = 1e-5
RMS_EPS = 1e-5

LANES = 128
SUBLANES = 8
VMEM_LIMIT_BYTES = 56 * 1024 * 1024


def _cparams(n_grid):
    return pltpu.CompilerParams(dimension_semantics=("arbitrary",) * n_grid,
                                vmem_limit_bytes=VMEM_LIMIT_BYTES)


def _tile(dim, pref, align=SUBLANES):
    if dim <= pref:
        return dim
    t = pref - pref % align
    while t >= align:
        if dim % t == 0:
            return t
        t -= align
    return dim


def _mm_body(*refs, rope_blocks, tn):
    if rope_blocks is None:
        x_ref, w_ref = refs[:2]
        outs = refs[2:]
    else:
        x_ref, w_ref, cos_ref, sin_ref = refs[:4]
        outs = refs[4:]
    acc = jnp.dot(x_ref[...], w_ref[...], preferred_element_type=F32)

    def plain():
        for o in outs:
            o[...] = acc.astype(o.dtype)

    if rope_blocks is None:
        plain()
        return

    j = pl.program_id(0)
    in_rope = jnp.logical_and(j >= rope_blocks[0], j < rope_blocks[1])

    @pl.when(jnp.logical_not(in_rope))
    def _():
        plain()

    @pl.when(in_rope)
    def _():
        cos = cos_ref[...]
        sin = sin_ref[...]
        for c in range(tn // HD):
            xc = acc[:, c * HD:(c + 1) * HD]
            rc = xc * cos + pltpu.roll(xc, HD // 2, 1) * sin
            for o in outs:
                o[:, c * HD:(c + 1) * HD] = rc.astype(o.dtype)


def _matmul(x, w, out_dtypes, *, tm, tn, rope=None):
    m, k = x.shape
    n = w.shape[1]
    tm = _tile(m, tm)
    tn = _tile(n if rope is None else math.gcd(n, rope[2], rope[3]), tn, LANES)
    grid = (n // tn, m // tm)
    in_specs = [pl.BlockSpec((tm, k), lambda j, i: (i, 0)),
                pl.BlockSpec((k, tn), lambda j, i: (0, j))]
    args = [x, w]
    rope_blocks = None
    if rope is not None:
        cos, sin, lo, hi, period = rope
        assert lo % tn == 0 and hi % tn == 0 and tn % HD == 0
        rope_blocks = (lo // tn, hi // tn)
        if period <= tm:
            assert tm % period == 0 or period == tm
            reps = tm // period
            cos = jnp.tile(cos, (reps, 1))
            sin = jnp.tile(sin, (reps, 1))
            tspec = pl.BlockSpec((tm, HD), lambda j, i: (0, 0))
        else:
            assert period % tm == 0
            nper = period // tm
            tspec = pl.BlockSpec((tm, HD), lambda j, i: (i % nper, 0))
        in_specs += [tspec, tspec]
        args += [cos, sin]
    out_shape = tuple(jax.ShapeDtypeStruct((m, n), dt) for dt in out_dtypes)
    out_specs = tuple(pl.BlockSpec((tm, tn), lambda j, i: (i, j)) for _ in out_dtypes)
    return pl.pallas_call(
        functools.partial(_mm_body, rope_blocks=rope_blocks, tn=tn),
        grid=grid, in_specs=in_specs, out_specs=out_specs, out_shape=out_shape,
        compiler_params=_cparams(2))(*args)


def _ln_body(x_ref, f_ref, g_ref, b_ref, of_ref, ob_ref, *, alpha):
    y = alpha * x_ref[...] + f_ref[...]
    mu = jnp.mean(y, axis=-1, keepdims=True)
    yc = y - mu
    var = jnp.mean(yc * yc, axis=-1, keepdims=True)
    out = yc * lax.rsqrt(var + LN_EPS) * g_ref[...] + b_ref[...]
    of_ref[...] = out
    ob_ref[...] = out.astype(BF16)


def _deepnorm_ln(x, f, g, b, alpha, *, tm=256):
    m, d = x.shape
    tm = _tile(m, tm)
    row = pl.BlockSpec((tm, d), lambda i: (i, 0))
    vec = pl.BlockSpec((1, d), lambda i: (0, 0))
    return pl.pallas_call(
        functools.partial(_ln_body, alpha=alpha),
        grid=(m // tm,), in_specs=[row, row, vec, vec], out_specs=(row, row),
        out_shape=(jax.ShapeDtypeStruct((m, d), F32), jax.ShapeDtypeStruct((m, d), BF16)),
        compiler_params=_cparams(1))(x, f, g.reshape(1, d), b.reshape(1, d))


CONV_CH_CHUNK = 256
CONV_ROW_CHUNK = 32
A_PAD = 8
B_PAD = 32
LN_ROW_CHUNK = 16


def _convmix_body(gb_ref, gc_ref, h_ref, ga_ref, gg_ref, ctxa_ref, ctxb_ref, wa_ref, wb_ref,
                  bb_ref, ng_ref, nb_ref, mix_ref, sa_ref, sb_ref, exta, extb, cbuf, *, tt, d):
    t = pl.program_id(1)
    a0 = A_PAD - (CONV_A_W - 1)
    b0 = B_PAD - (CONV_B_W - 1)

    @pl.when(t == 0)
    def _():
        exta[a0:A_PAD, :] = ctxa_ref[0]
        extb[b0:B_PAD, :] = ctxb_ref[0]

    exta[A_PAD:A_PAD + tt, :] = gc_ref[0] * h_ref[0]
    extb[B_PAD:B_PAD + tt, :] = ga_ref[0] * jax.nn.sigmoid(gg_ref[0])

    cw = min(CONV_CH_CHUNK, d)
    rc = min(CONV_ROW_CHUNK, tt)

    def chunk(cc, carry):
        c0 = pl.multiple_of(cc * cw, cw)
        cols = pl.ds(c0, cw)
        for r0 in range(0, tt, rc):
            acc = wa_ref[0:1, cols] * exta[a0 + r0:a0 + r0 + rc, cols]
            for k in range(1, CONV_A_W):
                acc = acc + wa_ref[k:k + 1, cols] * exta[a0 + r0 + k:a0 + r0 + k + rc, cols]
            mix_ref[0, r0:r0 + rc, cols] = (gb_ref[0, r0:r0 + rc, cols] * acc).astype(BF16)
            accb = wb_ref[0:1, cols] * extb[b0 + r0:b0 + r0 + rc, cols]
            for k in range(1, CONV_B_W):
                accb = accb + wb_ref[k:k + 1, cols] * extb[b0 + r0 + k:b0 + r0 + k + rc, cols]
            cbuf[r0:r0 + rc, cols] = accb + bb_ref[0:1, cols]
        return carry

    lax.fori_loop(0, d // cw, chunk, 0)

    lr = min(LN_ROW_CHUNK, tt)

    def ln_rows(i, carry):
        r0 = pl.multiple_of(i * lr, lr)
        c = cbuf[pl.ds(r0, lr), :]
        mu = jnp.mean(c, axis=-1, keepdims=True)
        cc = c - mu
        var = jnp.mean(cc * cc, axis=-1, keepdims=True)
        y = cc * lax.rsqrt(var + LN_EPS) * ng_ref[...] + nb_ref[...]
        mix_ref[0, pl.ds(r0, lr), d:2 * d] = (y * jax.nn.sigmoid(y)).astype(BF16)
        return carry

    lax.fori_loop(0, tt // lr, ln_rows, 0)

    ta = exta[a0 + tt:A_PAD + tt, :]
    tb = extb[b0 + tt:B_PAD + tt, :]
    exta[a0:A_PAD, :] = ta
    extb[b0:B_PAD, :] = tb

    @pl.when(t == pl.num_programs(1) - 1)
    def _():
        sa_ref[0] = ta
        sb_ref[0] = tb


def _conv_mixer_prompt(proj, ctx_a, ctx_b, wa, wb, bb, ng, nb, *, tt=256):
    bsz, t, _ = proj.shape
    d = wa.shape[1]
    tt = _tile(t, tt)
    assert tt >= CONV_B_W - 1 and tt % min(CONV_ROW_CHUNK, tt) == 0 and d % min(CONV_CH_CHUNK, d) == 0
    col = lambda c: pl.BlockSpec((1, tt, d), lambda b, i, c=c: (b, i, c))
    full = lambda r: pl.BlockSpec((r, d), lambda b, i: (0, 0))
    st = lambda r: pl.BlockSpec((1, r, d), lambda b, i: (b, 0, 0))
    return pl.pallas_call(
        functools.partial(_convmix_body, tt=tt, d=d),
        grid=(bsz, t // tt),
        in_specs=[col(0), col(1), col(2), col(3), col(4), st(CONV_A_W - 1), st(CONV_B_W - 1),
                  full(CONV_A_W), full(CONV_B_W), full(1), full(1), full(1)],
        out_specs=(pl.BlockSpec((1, tt, 2 * d), lambda b, i: (b, i, 0)), st(CONV_A_W - 1), st(CONV_B_W - 1)),
        out_shape=(jax.ShapeDtypeStruct((bsz, t, 2 * d), BF16),
                   jax.ShapeDtypeStruct((bsz, CONV_A_W - 1, d), F32),
                   jax.ShapeDtypeStruct((bsz, CONV_B_W - 1, d), F32)),
        scratch_shapes=[pltpu.VMEM((A_PAD + tt, d), F32), pltpu.VMEM((B_PAD + tt, d), F32),
                        pltpu.VMEM((tt, d), F32)],
        compiler_params=_cparams(2),
    )(proj, proj, proj, proj, proj, ctx_a, ctx_b, wa, wb, bb.reshape(1, d), ng.reshape(1, d), nb.reshape(1, d))


def _convmix_dec_body(proj_ref, ctxa_ref, ctxb_ref, wa_ref, wb_ref, bb_ref, ng_ref, nb_ref,
                      mix_ref, sa_ref, sb_ref, *, d):
    gate_b = proj_ref[:, 0:d]
    u = proj_ref[:, d:2 * d] * proj_ref[:, 2 * d:3 * d]
    glu = proj_ref[:, 3 * d:4 * d] * jax.nn.sigmoid(proj_ref[:, 4 * d:5 * d])
    acc = wa_ref[CONV_A_W - 1:CONV_A_W, :] * u
    for k in range(CONV_A_W - 1):
        acc = acc + wa_ref[k:k + 1, :] * ctxa_ref[k]
    mix_ref[:, 0:d] = (gate_b * acc).astype(BF16)
    accb = wb_ref[CONV_B_W - 1:CONV_B_W, :] * glu + bb_ref[...]
    for k in range(CONV_B_W - 1):
        accb = accb + wb_ref[k:k + 1, :] * ctxb_ref[k]
    mu = jnp.mean(accb, axis=-1, keepdims=True)
    cc = accb - mu
    var = jnp.mean(cc * cc, axis=-1, keepdims=True)
    y = cc * lax.rsqrt(var + LN_EPS) * ng_ref[...] + nb_ref[...]
    mix_ref[:, d:2 * d] = (y * jax.nn.sigmoid(y)).astype(BF16)
    for k in range(CONV_A_W - 2):
        sa_ref[k] = ctxa_ref[k + 1]
    sa_ref[CONV_A_W - 2] = u
    for k in range(CONV_B_W - 2):
        sb_ref[k] = ctxb_ref[k + 1]
    sb_ref[CONV_B_W - 2] = glu


def _conv_mixer_decode(proj, ctx_a_t, ctx_b_t, wa, wb, bb, ng, nb):
    s = proj.shape[0]
    d = wa.shape[1]
    return pl.pallas_call(
        functools.partial(_convmix_dec_body, d=d),
        out_shape=(jax.ShapeDtypeStruct((s, 2 * d), BF16),
                   jax.ShapeDtypeStruct((CONV_A_W - 1, s, d), F32),
                   jax.ShapeDtypeStruct((CONV_B_W - 1, s, d), F32)),
        compiler_params=pltpu.CompilerParams(vmem_limit_bytes=VMEM_LIMIT_BYTES),
    )(proj, ctx_a_t, ctx_b_t, wa, wb, bb.reshape(1, d), ng.reshape(1, d), nb.reshape(1, d))


FFN_PAD = 8


def _gelu(x):
    return 0.5 * x * (1.0 + lax.erf(x * (1.0 / math.sqrt(2.0))))


def _ffn_up_body(x_ref, wg_ref, wv_ref, cw_ref, cb_ref, ctx_ref, h_ref, st_ref, ext, vbuf, *, tm):
    t = pl.program_id(2)
    e0 = FFN_PAD - (FFN_CONV_W - 1)

    @pl.when(t == 0)
    def _():
        ext[e0:FFN_PAD, :] = ctx_ref[0]

    x = x_ref[0]
    ext[FFN_PAD:FFN_PAD + tm, :] = jnp.dot(x, wg_ref[...], preferred_element_type=F32)
    vbuf[...] = jnp.dot(x, wv_ref[...], preferred_element_type=F32)

    rc = min(CONV_ROW_CHUNK, tm)
    for r0 in range(0, tm, rc):
        acc = cw_ref[0:1, :] * ext[e0 + r0:e0 + r0 + rc, :] + cb_ref[...]
        for k in range(1, FFN_CONV_W):
            acc = acc + cw_ref[k:k + 1, :] * ext[e0 + r0 + k:e0 + r0 + k + rc, :]
        h_ref[0, r0:r0 + rc, :] = (_gelu(acc) * vbuf[r0:r0 + rc, :]).astype(BF16)

    tail = ext[e0 + tm:FFN_PAD + tm, :]
    ext[e0:FFN_PAD, :] = tail

    @pl.when(t == pl.num_programs(2) - 1)
    def _():
        st_ref[0] = tail


def _ffn_up_prompt(x, w_up, cw, cb, ctx, *, tm=512, tn=256):
    bsz, t, k = x.shape
    f = cw.shape[1]
    tm = _tile(t, tm)
    tn = _tile(f, tn, LANES)
    nc = f // tn
    assert tm % min(CONV_ROW_CHUNK, tm) == 0
    return pl.pallas_call(
        functools.partial(_ffn_up_body, tm=tm),
        grid=(nc, bsz, t // tm),
        in_specs=[pl.BlockSpec((1, tm, k), lambda c, b, i: (b, i, 0)),
                  pl.BlockSpec((k, tn), lambda c, b, i: (0, c)),
                  pl.BlockSpec((k, tn), lambda c, b, i: (0, nc + c)),
                  pl.BlockSpec((FFN_CONV_W, tn), lambda c, b, i: (0, c)),
                  pl.BlockSpec((1, tn), lambda c, b, i: (0, c)),
                  pl.BlockSpec((1, FFN_CONV_W - 1, tn), lambda c, b, i: (b, 0, c))],
        out_specs=(pl.BlockSpec((1, tm, tn), lambda c, b, i: (b, i, c)),
                   pl.BlockSpec((1, FFN_CONV_W - 1, tn), lambda c, b, i: (b, 0, c))),
        out_shape=(jax.ShapeDtypeStruct((bsz, t, f), BF16),
                   jax.ShapeDtypeStruct((bsz, FFN_CONV_W - 1, f), F32)),
        scratch_shapes=[pltpu.VMEM((FFN_PAD + tm, tn), F32), pltpu.VMEM((tm, tn), F32)],
        compiler_params=_cparams(3),
    )(x, w_up, w_up, cw, cb.reshape(1, f), ctx)


def _ffn_up_dec_body(x_ref, wg_ref, wv_ref, cw_ref, cb_ref, ctx_ref, h_ref, st_ref):
    x = x_ref[...]
    g = jnp.dot(x, wg_ref[...], preferred_element_type=F32)
    v = jnp.dot(x, wv_ref[...], preferred_element_type=F32)
    acc = cw_ref[FFN_CONV_W - 1:FFN_CONV_W, :] * g + cb_ref[...]
    for k in range(FFN_CONV_W - 1):
        acc = acc + cw_ref[k:k + 1, :] * ctx_ref[k]
    h_ref[...] = (_gelu(acc) * v).astype(BF16)
    for k in range(FFN_CONV_W - 2):
        st_ref[k] = ctx_ref[k + 1]
    st_ref[FFN_CONV_W - 2] = g


def _ffn_up_decode(x, w_up, cw, cb, ctx_t, *, tn=512):
    s, k = x.shape
    f = cw.shape[1]
    tn = _tile(f, tn, LANES)
    nc = f // tn
    return pl.pallas_call(
        _ffn_up_dec_body,
        grid=(nc,),
        in_specs=[pl.BlockSpec((s, k), lambda c: (0, 0)),
                  pl.BlockSpec((k, tn), lambda c: (0, c)),
                  pl.BlockSpec((k, tn), lambda c: (0, nc + c)),
                  pl.BlockSpec((FFN_CONV_W, tn), lambda c: (0, c)),
                  pl.BlockSpec((1, tn), lambda c: (0, c)),
                  pl.BlockSpec((FFN_CONV_W - 1, s, tn), lambda c: (0, 0, c))],
        out_specs=(pl.BlockSpec((s, tn), lambda c: (0, c)),
                   pl.BlockSpec((FFN_CONV_W - 1, s, tn), lambda c: (0, 0, c))),
        out_shape=(jax.ShapeDtypeStruct((s, f), BF16),
                   jax.ShapeDtypeStruct((FFN_CONV_W - 1, s, f), F32)),
        compiler_params=_cparams(1),
    )(x, w_up, w_up, cw, cb.reshape(1, f), ctx_t)


def _qk(q, k):
    return lax.dot_general(q, k, (((1,), (1,)), ((), ())), preferred_element_type=F32)


def _softplus(z):
    return jnp.maximum(z, 0.0) + jnp.log1p(jnp.exp(-jnp.abs(z)))


def _suffix_sums(lk, tri):
    hi = lk.astype(BF16)
    lo = (lk - hi.astype(F32)).astype(BF16)
    return (jnp.dot(hi, tri, preferred_element_type=F32) + jnp.dot(lo, tri, preferred_element_type=F32))


def _strict_lower_ones(n):
    r = lax.broadcasted_iota(jnp.int32, (n, n), 0)
    c = lax.broadcasted_iota(jnp.int32, (n, n), 1)
    return jnp.where(r > c, 1.0, 0.0).astype(BF16)


def _lambda(lq1_ref, lk1_ref, lq2_ref, lk2_ref, lambda_init):
    s1 = jnp.sum(lq1_ref[...] * lk1_ref[...], axis=-1, keepdims=True)
    s2 = jnp.sum(lq2_ref[...] * lk2_ref[...], axis=-1, keepdims=True)
    return jnp.exp(s1) - jnp.exp(s2) + lambda_init


def _sb_prompt_body(q_ref, k_ref, v_ref, o_ref, acc_ref, c_ref, *, tq, scale):
    qi = pl.program_id(2)
    q = q_ref[0]
    tri = _strict_lower_ones(tq)
    row = lax.broadcasted_iota(jnp.int32, (tq, tq), 0)
    col = lax.broadcasted_iota(jnp.int32, (tq, tq), 1)

    def block(kb, masked):
        ks = pl.ds(pl.multiple_of(kb * tq, tq), tq)
        z = _qk(q, k_ref[0, ks, :]) * scale
        sp = _softplus(z)
        lk = jnp.where(col < row, -sp, 0.0) if masked else -sp
        within = _suffix_sums(lk, tri)
        a = jnp.exp(z - sp + (c_ref[...] + within))
        if masked:
            a = jnp.where(col < row, a, 0.0)
        c_ref[...] = c_ref[...] + within[:, 0:1] + lk[:, 0:1]
        return jnp.dot(a.astype(BF16), v_ref[0, ks, :], preferred_element_type=F32)

    c_ref[...] = jnp.zeros_like(c_ref)
    acc_ref[...] = block(qi, True)

    def body(i, carry):
        acc_ref[...] = acc_ref[...] + block(qi - 1 - i, False)
        return carry

    lax.fori_loop(0, qi, body, 0)
    o_ref[0] = acc_ref[...].astype(o_ref.dtype)


def _sb_attention_prompt(qkv, n_heads, *, tq=256):
    bsz, t, _ = qkv.shape
    tq = _tile(t, tq)
    return pl.pallas_call(
        functools.partial(_sb_prompt_body, tq=tq, scale=HD ** -0.5),
        grid=(bsz, n_heads, t // tq),
        in_specs=[pl.BlockSpec((1, tq, HD), lambda b, h, i: (b, i, h)),
                  pl.BlockSpec((1, t, HD), lambda b, h, i: (b, 0, n_heads + h)),
                  pl.BlockSpec((1, t, HD), lambda b, h, i: (b, 0, 2 * n_heads + h))],
        out_specs=pl.BlockSpec((1, tq, HD), lambda b, h, i: (b, i, h)),
        out_shape=jax.ShapeDtypeStruct((bsz, t, n_heads * HD), BF16),
        scratch_shapes=[pltpu.VMEM((tq, HD), F32), pltpu.VMEM((tq, 1), F32)],
        compiler_params=_cparams(3),
    )(qkv, qkv, qkv)


def _df_prompt_body(q_ref, k_ref, v_ref, lq1_ref, lk1_ref, lq2_ref, lk2_ref, g_ref, o_ref,
                    m_ref, l_ref, acc_ref, *, tq, scale, lambda_init):
    qi = pl.program_id(2)
    row = lax.broadcasted_iota(jnp.int32, (tq, tq), 0)
    col = lax.broadcasted_iota(jnp.int32, (tq, tq), 1)

    def block(kb, masked):
        ks = pl.ds(pl.multiple_of(kb * tq, tq), tq)
        v = v_ref[0, ks, :]
        for c in range(2):
            s = _qk(q_ref[0, :, c * HD:(c + 1) * HD], k_ref[0, ks, c * HD:(c + 1) * HD]) * scale
            if masked:
                s = jnp.where(col <= row, s, -jnp.inf)
            m_old = m_ref[c]
            m_new = jnp.maximum(m_old, jnp.max(s, axis=-1, keepdims=True))
            alpha = jnp.exp(m_old - m_new)
            p = jnp.exp(s - m_new)
            l_ref[c] = alpha * l_ref[c] + jnp.sum(p, axis=-1, keepdims=True)
            acc_ref[c] = alpha * acc_ref[c] + jnp.dot(p.astype(BF16), v, preferred_element_type=F32)
            m_ref[c] = m_new

    m_ref[...] = jnp.full_like(m_ref, -jnp.inf)
    l_ref[...] = jnp.zeros_like(l_ref)
    acc_ref[...] = jnp.zeros_like(acc_ref)
    block(qi, True)

    def body(i, carry):
        block(qi - 1 - i, False)
        return carry

    lax.fori_loop(0, qi, body, 0)

    lam = _lambda(lq1_ref, lk1_ref, lq2_ref, lk2_ref, lambda_init)
    o = acc_ref[0] / l_ref[0] - lam * (acc_ref[1] / l_ref[1])
    y = o * lax.rsqrt(jnp.mean(o * o, axis=-1, keepdims=True) + RMS_EPS) * g_ref[...]
    o_ref[0] = (y * (1.0 - lambda_init)).astype(o_ref.dtype)


def _df_attention_prompt(qkv, col0, n_heads, lam_params, subln_g, lambda_init, *, tq=256):
    bsz, t, _ = qkv.shape
    tq = _tile(t, tq)
    w = 2 * HD
    c0 = col0 // w
    vec = pl.BlockSpec((1, HD), lambda b, h, i: (0, 0))
    return pl.pallas_call(
        functools.partial(_df_prompt_body, tq=tq, scale=HD ** -0.5, lambda_init=lambda_init),
        grid=(bsz, n_heads, t // tq),
        in_specs=[pl.BlockSpec((1, tq, w), lambda b, h, i: (b, i, c0 + h)),
                  pl.BlockSpec((1, t, w), lambda b, h, i: (b, 0, c0 + n_heads + h)),
                  pl.BlockSpec((1, t, w), lambda b, h, i: (b, 0, c0 + 2 * n_heads + h)),
                  vec, vec, vec, vec, pl.BlockSpec((1, w), lambda b, h, i: (0, 0))],
        out_specs=pl.BlockSpec((1, tq, w), lambda b, h, i: (b, i, h)),
        out_shape=jax.ShapeDtypeStruct((bsz, t, n_heads * w), BF16),
        scratch_shapes=[pltpu.VMEM((2, tq, 1), F32), pltpu.VMEM((2, tq, 1), F32), pltpu.VMEM((2, tq, w), F32)],
        compiler_params=_cparams(3),
    )(qkv, qkv, qkv, *[p.reshape(1, HD) for p in lam_params], subln_g.reshape(1, w))


def _sb_decode_body(pt_ref, q_ref, k_ref, v_ref, o_ref, acc_ref, c_ref, *, n_heads, page, scale):
    p = pl.program_id(1)

    @pl.when(p == 0)
    def _():
        acc_ref[...] = jnp.zeros_like(acc_ref)
        c_ref[...] = jnp.zeros_like(c_ref)

    q = q_ref[0]
    head = lax.broadcasted_iota(jnp.int32, (n_heads, page), 0)
    z = jnp.zeros((n_heads, page), F32)
    for h in range(n_heads):
        z = jnp.where(head == h, _qk(q, k_ref[:, h, :].astype(BF16)), z)
    z = z * scale
    sp = _softplus(z)
    lk = -sp
    within = _suffix_sums(lk, _strict_lower_ones(page))
    a = jnp.exp(z - sp + (c_ref[...] + within)).astype(BF16)
    c_ref[...] = c_ref[...] + within[:, 0:1] + lk[:, 0:1]
    head_o = lax.broadcasted_iota(jnp.int32, (n_heads, HD), 0)
    upd = jnp.zeros((n_heads, HD), F32)
    for h in range(n_heads):
        upd = jnp.where(head_o == h, jnp.dot(a, v_ref[:, h, :].astype(BF16), preferred_element_type=F32), upd)
    acc_ref[...] = acc_ref[...] + upd

    @pl.when(p == pl.num_programs(1) - 1)
    def _():
        o_ref[0] = acc_ref[...].astype(o_ref.dtype)


def _sb_attention_decode(q, cache_k, cache_v, page_table, layer):
    s, n_heads, _ = q.shape
    page = cache_k.shape[2]
    n_pages = page_table.shape[1]
    kv = pl.BlockSpec((None, None, page, n_heads, HD),
                      lambda b, p, pt: (pt[b, n_pages - 1 - p], layer, 0, 0, 0))
    hd = pl.BlockSpec((1, n_heads, HD), lambda b, p, pt: (b, 0, 0))
    return pl.pallas_call(
        functools.partial(_sb_decode_body, n_heads=n_heads, page=page, scale=HD ** -0.5),
        grid_spec=pltpu.PrefetchScalarGridSpec(
            num_scalar_prefetch=1, grid=(s, n_pages), in_specs=[hd, kv, kv], out_specs=hd,
            scratch_shapes=[pltpu.VMEM((n_heads, HD), F32), pltpu.VMEM((n_heads, 1), F32)]),
        out_shape=jax.ShapeDtypeStruct((s, n_heads, HD), BF16),
        compiler_params=_cparams(2),
    )(page_table, q, cache_k, cache_v)


def _df_decode_body(pt_ref, q_ref, kn_ref, vn_ref, k_ref, v_ref, lq1_ref, lk1_ref, lq2_ref, lk2_ref, g_ref,
                    o_ref, m_ref, l_ref, acc_ref, *, n_heads, page, scale, lambda_init):
    p = pl.program_id(1)
    nr = 2 * n_heads

    @pl.when(p == 0)
    def _():
        m_ref[...] = jnp.full_like(m_ref, -jnp.inf)
        l_ref[...] = jnp.zeros_like(l_ref)
        acc_ref[...] = jnp.zeros_like(acc_ref)

    q = q_ref[0]
    rows = lax.broadcasted_iota(jnp.int32, (nr, page), 0)
    s = jnp.zeros((nr, page), F32)
    for c in range(2):
        for h in range(n_heads):
            s = jnp.where(rows == c * n_heads + h, _qk(q, k_ref[:, 2 * h + c, :].astype(BF16)), s)
    s = s * scale
    m_old = m_ref[...]
    m_new = jnp.maximum(m_old, jnp.max(s, axis=-1, keepdims=True))
    alpha = jnp.exp(m_old - m_new)
    pr = jnp.exp(s - m_new)
    l_ref[...] = alpha * l_ref[...] + jnp.sum(pr, axis=-1, keepdims=True)
    prb = pr.astype(BF16)
    rows_o = lax.broadcasted_iota(jnp.int32, (nr, 2 * HD), 0)
    upd = jnp.zeros((nr, 2 * HD), F32)
    for h in range(n_heads):
        r = jnp.dot(prb, v_ref[:, h, :].astype(BF16), preferred_element_type=F32)
        upd = jnp.where(jnp.logical_or(rows_o == h, rows_o == n_heads + h), r, upd)
    acc_ref[...] = alpha * acc_ref[...] + upd
    m_ref[...] = m_new

    @pl.when(p == pl.num_programs(1) - 1)
    def _():
        s_new = jnp.sum(q.astype(F32) * kn_ref[0], axis=-1, keepdims=True) * scale
        m_fin = jnp.maximum(m_ref[...], s_new)
        a_fin = jnp.exp(m_ref[...] - m_fin)
        p_new = jnp.exp(s_new - m_fin)
        l_fin = a_fin * l_ref[...] + p_new
        o2 = (a_fin * acc_ref[...] + p_new * vn_ref[0]) / l_fin
        lam = _lambda(lq1_ref, lk1_ref, lq2_ref, lk2_ref, lambda_init)
        o = o2[0:n_heads] - lam * o2[n_heads:nr]
        y = o * lax.rsqrt(jnp.mean(o * o, axis=-1, keepdims=True) + RMS_EPS) * g_ref[...]
        o_ref[0] = (y * (1.0 - lambda_init)).astype(o_ref.dtype)


def _df_attention_decode(q, k_new, v_new, cache_k, cache_v, page_table, layer, lam_params, subln_g, lambda_init):
    s, nr, _ = q.shape
    n_heads = nr // 2
    page = cache_k.shape[2]
    n_pages = page_table.shape[1]
    w = 2 * HD
    kspec = pl.BlockSpec((None, None, page, nr, HD), lambda b, p, pt: (pt[b, p], layer, 0, 0, 0))
    vspec = pl.BlockSpec((None, None, page, n_heads, w), lambda b, p, pt: (pt[b, p], layer, 0, 0, 0))
    qspec = pl.BlockSpec((1, nr, HD), lambda b, p, pt: (b, 0, 0))
    vnspec = pl.BlockSpec((1, nr, w), lambda b, p, pt: (b, 0, 0))
    vec = pl.BlockSpec((1, HD), lambda b, p, pt: (0, 0))
    return pl.pallas_call(
        functools.partial(_df_decode_body, n_heads=n_heads, page=page, scale=HD ** -0.5, lambda_init=lambda_init),
        grid_spec=pltpu.PrefetchScalarGridSpec(
            num_scalar_prefetch=1, grid=(s, n_pages),
            in_specs=[qspec, qspec, vnspec, kspec, vspec, vec, vec, vec, vec,
                      pl.BlockSpec((1, w), lambda b, p, pt: (0, 0))],
            out_specs=pl.BlockSpec((1, n_heads, w), lambda b, p, pt: (b, 0, 0)),
            scratch_shapes=[pltpu.VMEM((nr, 1), F32), pltpu.VMEM((nr, 1), F32), pltpu.VMEM((nr, w), F32)]),
        out_shape=jax.ShapeDtypeStruct((s, n_heads, w), BF16),
        compiler_params=_cparams(2),
    )(page_table, q, k_new, v_new, cache_k, cache_v, *[p.reshape(1, HD) for p in lam_params],
      subln_g.reshape(1, w))


def _rope_tables(positions):
    half = HD // 2
    inv_freq = ROPE_THETA ** (-jnp.arange(half, dtype=F32) / half)
    ang = positions.astype(F32)[:, None] * inv_freq[None, :]
    cos, sin = jnp.cos(ang), jnp.sin(ang)
    return jnp.concatenate([cos, cos], axis=-1), jnp.concatenate([-sin, sin], axis=-1)


def kernel(x_prompt, x_sample, state_conv_a, state_conv_b, state_ffn, cache_sb_k, cache_sb_v, cache_df_k, cache_df_v, page_table, w_in_conv, conv_a_w, conv_b_w, conv_b_b, norm_b_g, norm_b_b, w_out_conv, w_in_attn, lambda_q1, lambda_k1, lambda_q2, lambda_k2, subln_g, w_out_attn, ln1_g, ln1_b, ln2_g, ln2_b, w_ffn_up, ffn_conv_w, ffn_conv_b, w_ffn_down):
    bp, t, dm = x_prompt.shape
    bs = x_sample.shape[0]
    assert x_sample.shape[1] == 1
    depth = ln1_g.shape[0]
    d_a = conv_a_w.shape[2]
    d_ff = ffn_conv_w.shape[2]
    h_sb = cache_sb_k.shape[3]
    h_df = cache_df_v.shape[3]
    d_sb = h_sb * HD
    d_df = h_df * 2 * HD
    page = cache_sb_k.shape[2]
    past_len = page_table.shape[1] * page
    alpha = (2 * depth) ** 0.25
    mp = bp * t

    hp = x_prompt.reshape(mp, dm)
    hs = x_sample.reshape(bs, dm)
    hp_b = hp.astype(BF16)
    hs_b = hs.astype(BF16)

    ca_p, ca_s, cb_p, cb_s, f_p, f_s = [], [], [], [], [], []
    rows_p = ([], [], [], [])
    rows_s = ([], [], [], [])

    for layer in range(depth):
        i = layer // 2
        if layer % 2 == 0:
            w_in = w_in_conv[i].astype(BF16)
            w_out = w_out_conv[i].astype(BF16)
            cw = (conv_a_w[i], conv_b_w[i], conv_b_b[i], norm_b_g[i], norm_b_b[i])
            (proj_p,) = _matmul(hp_b, w_in, (F32,), tm=1024, tn=1024)
            mixed_p, sa, sb = _conv_mixer_prompt(
                proj_p.reshape(bp, t, 5 * d_a), jnp.zeros((bp, CONV_A_W - 1, d_a), F32),
                jnp.zeros((bp, CONV_B_W - 1, d_a), F32), *cw)
            ca_p.append(sa)
            cb_p.append(sb)
            (m_p,) = _matmul(mixed_p.reshape(mp, 2 * d_a), w_out, (F32,), tm=1024, tn=1024)

            (proj_s,) = _matmul(hs_b, w_in, (F32,), tm=bs, tn=1024)
            mixed_s, sa, sb = _conv_mixer_decode(
                proj_s, jnp.swapaxes(state_conv_a[i], 0, 1), jnp.swapaxes(state_conv_b[i], 0, 1), *cw)
            ca_s.append(jnp.swapaxes(sa, 0, 1))
            cb_s.append(jnp.swapaxes(sb, 0, 1))
            (m_s,) = _matmul(mixed_s, w_out, (F32,), tm=bs, tn=1024)
        else:
            lambda_init = 0.8 - 0.6 * math.exp(-0.3 * layer)
            w_in = w_in_attn[i].astype(BF16)
            w_out = w_out_attn[i].astype(BF16)
            lam_params = (lambda_q1[i], lambda_k1[i], lambda_q2[i], lambda_k2[i])
            rope_lo, rope_hi = 3 * d_sb, 3 * d_sb + 2 * d_df
            kcols = (slice(d_sb, 2 * d_sb), slice(2 * d_sb, 3 * d_sb),
                     slice(3 * d_sb + d_df, 3 * d_sb + 2 * d_df), slice(3 * d_sb + 2 * d_df, 3 * d_sb + 3 * d_df))

            cos_p, sin_p = _rope_tables(jnp.arange(t, dtype=jnp.int32))
            qkv_f, qkv_b = _matmul(hp_b, w_in, (F32, BF16), tm=512, tn=1024,
                                   rope=(cos_p, sin_p, rope_lo, rope_hi, t))
            new_p = (qkv_f[:, kcols[0]].reshape(bp, t, h_sb, HD), qkv_f[:, kcols[1]].reshape(bp, t, h_sb, HD),
                     qkv_f[:, kcols[2]].reshape(bp, t, h_df, 2, HD), qkv_f[:, kcols[3]].reshape(bp, t, h_df, 2 * HD))
            qkv_b3 = qkv_b.reshape(bp, t, -1)
            o_sb = _sb_attention_prompt(qkv_b3, h_sb)
            o_df = _df_attention_prompt(qkv_b3, 3 * d_sb, h_df, lam_params, subln_g[i], lambda_init)
            mixed_p = jnp.concatenate([o_sb, o_df], axis=-1).reshape(mp, d_sb + d_df)
            (m_p,) = _matmul(mixed_p, w_out, (F32,), tm=1024, tn=1024)

            cos_s, sin_s = _rope_tables(jnp.full((1,), past_len, jnp.int32))
            qkv_sf, qkv_sb16 = _matmul(hs_b, w_in, (F32, BF16), tm=bs, tn=1024,
                                       rope=(cos_s, sin_s, rope_lo, rope_hi, 1))
            new_s = (qkv_sf[:, kcols[0]].reshape(bs, 1, h_sb, HD), qkv_sf[:, kcols[1]].reshape(bs, 1, h_sb, HD),
                     qkv_sf[:, kcols[2]].reshape(bs, 1, h_df, 2, HD), qkv_sf[:, kcols[3]].reshape(bs, 1, h_df, 2 * HD))
            o_sb_s = _sb_attention_decode(qkv_sb16[:, 0:d_sb].reshape(bs, h_sb, HD),
                                          cache_sb_k, cache_sb_v, page_table, i)
            q_df = jnp.swapaxes(qkv_sb16[:, 3 * d_sb:3 * d_sb + d_df].reshape(bs, h_df, 2, HD), 1, 2)
            k_df = jnp.swapaxes(new_s[2].reshape(bs, h_df, 2, HD), 1, 2)
            v_df = jnp.tile(new_s[3].reshape(bs, h_df, 2 * HD), (1, 2, 1))
            o_df_s = _df_attention_decode(
                q_df.reshape(bs, 2 * h_df, HD), k_df.reshape(bs, 2 * h_df, HD), v_df,
                cache_df_k.reshape(cache_df_k.shape[:3] + (2 * h_df, HD)), cache_df_v, page_table, i,
                lam_params, subln_g[i], lambda_init)
            mixed_s = jnp.concatenate([o_sb_s.reshape(bs, d_sb), o_df_s.reshape(bs, d_df)], axis=-1)
            (m_s,) = _matmul(mixed_s, w_out, (F32,), tm=bs, tn=1024)
            for lst, r in zip(rows_p, new_p):
                lst.append(r)
            for lst, r in zip(rows_s, new_s):
                lst.append(r)

        hp, hp_b = _deepnorm_ln(hp, m_p, ln1_g[layer], ln1_b[layer], alpha)
        hs, hs_b = _deepnorm_ln(hs, m_s, ln1_g[layer], ln1_b[layer], alpha)

        w_up = w_ffn_up[layer].astype(BF16)
        w_down = w_ffn_down[layer].astype(BF16)
        h_p, st_p = _ffn_up_prompt(hp_b.reshape(bp, t, dm), w_up, ffn_conv_w[layer], ffn_conv_b[layer],
                                   jnp.zeros((bp, FFN_CONV_W - 1, d_ff), F32))
        f_p.append(st_p)
        (fp,) = _matmul(h_p.reshape(mp, d_ff), w_down, (F32,), tm=512, tn=512)
        h_s, st_s = _ffn_up_decode(hs_b, w_up, ffn_conv_w[layer], ffn_conv_b[layer],
                                   jnp.swapaxes(state_ffn[layer], 0, 1))
        f_s.append(jnp.swapaxes(st_s, 0, 1))
        (fs,) = _matmul(h_s, w_down, (F32,), tm=bs, tn=512)

        hp, hp_b = _deepnorm_ln(hp, fp, ln2_g[layer], ln2_b[layer], alpha)
        hs, hs_b = _deepnorm_ln(hs, fs, ln2_g[layer], ln2_b[layer], alpha)

    return (hp.reshape(bp, t, dm), hs.reshape(bs, 1, dm),
            jnp.stack(ca_p), jnp.stack(ca_s), jnp.stack(cb_p), jnp.stack(cb_s),
            jnp.stack(f_p), jnp.stack(f_s),
            jnp.stack(rows_p[0], axis=1), jnp.stack(rows_s[0], axis=1),
            jnp.stack(rows_p[1], axis=1), jnp.stack(rows_s[1], axis=1),
            jnp.stack(rows_p[2], axis=1), jnp.stack(rows_s[2], axis=1),
            jnp.stack(rows_p[3], axis=1), jnp.stack(rows_s[3], axis=1))
```

```python
import functools
import math

import jax
import jax.numpy as jnp
from jax import lax
from jax.experimental import pallas as pl
from jax.experimental.pallas import tpu as pltpu

F32 = jnp.float32
BF16 = jnp.bfloat16

HD = 128
CONV_A_W = 3
CONV_B_W = 31
FFN_CONV_W = 3
ROPE_THETA = 10000.0
LN_EPS = 1e-5
RMS_EPS = 1e-5

LANES = 128
SUBLANES = 8
VMEM_LIMIT_BYTES = 56 * 1024 * 1024


def _cparams(n_grid):
    return pltpu.CompilerParams(dimension_semantics=("arbitrary",) * n_grid,
                                vmem_limit_bytes=VMEM_LIMIT_BYTES)


def _tile(dim, pref, align=SUBLANES):
    if dim <= pref:
        return dim
    t = pref - pref % align
    while t >= align:
        if dim % t == 0:
            return t
        t -= align
    return dim


PROJ_TM = 1024
PROJ_TN = 512


def _mm_body(*refs, n_x, n_out, rope, cast_w, tn):
    x_refs = refs[:n_x]
    w_ref = refs[n_x]
    pos = n_x + 1
    if rope:
        cos_ref, sin_ref = refs[pos:pos + 2]
        pos += 2
    outs = refs[pos:pos + n_out]
    if cast_w:
        wb_ref = refs[pos + n_out]

        @pl.when(pl.program_id(1) == 0)
        def _():
            wb_ref[...] = w_ref[...].astype(BF16)

        w_ref = wb_ref
    acc = None
    k0 = 0
    for x_ref in x_refs:
        kk = x_ref.shape[1]
        part = jnp.dot(x_ref[...], w_ref[k0:k0 + kk, :], preferred_element_type=F32)
        acc = part if acc is None else acc + part
        k0 += kk
    if not rope:
        for o in outs:
            o[...] = acc.astype(o.dtype)
        return
    cos = cos_ref[...]
    sin = sin_ref[...]
    for c in range(tn // HD):
        xc = acc[:, c * HD:(c + 1) * HD]
        rc = xc * cos + pltpu.roll(xc, HD // 2, 1) * sin
        for o in outs:
            o[:, c * HD:(c + 1) * HD] = rc.astype(o.dtype)


def _matmul(xs, w, out_dtypes, *, tm, tn, cols=None, rope=None):
    xs = tuple(xs)
    m = xs[0].shape[0]
    k = sum(x.shape[1] for x in xs)
    assert k == w.shape[0]
    lo, hi = cols if cols is not None else (0, w.shape[1])
    n = hi - lo
    tm = _tile(m, tm)
    tn = _tile(math.gcd(n, lo) if lo else n, tn, LANES)
    assert lo % tn == 0 and n % tn == 0
    j0 = lo // tn
    cast_w = w.dtype != BF16
    in_specs = [pl.BlockSpec((tm, x.shape[1]), lambda j, i: (i, 0)) for x in xs]
    in_specs.append(pl.BlockSpec((k, tn), lambda j, i: (0, j0 + j)))
    args = list(xs) + [w]
    if rope is not None:
        cos, sin, period = rope
        assert tn % HD == 0
        if period <= tm:
            assert tm % period == 0
            cos = jnp.tile(cos, (tm // period, 1))
            sin = jnp.tile(sin, (tm // period, 1))
            tspec = pl.BlockSpec((tm, HD), lambda j, i: (0, 0))
        else:
            assert period % tm == 0
            nper = period // tm
            tspec = pl.BlockSpec((tm, HD), lambda j, i: (i % nper, 0))
        in_specs += [tspec, tspec]
        args += [cos, sin]
    out_shape = tuple(jax.ShapeDtypeStruct((m, n), dt) for dt in out_dtypes)
    out_specs = tuple(pl.BlockSpec((tm, tn), lambda j, i: (i, j)) for _ in out_dtypes)
    return pl.pallas_call(
        functools.partial(_mm_body, n_x=len(xs), n_out=len(out_dtypes), rope=rope is not None,
                          cast_w=cast_w, tn=tn),
        grid=(n // tn, m // tm), in_specs=in_specs, out_specs=out_specs, out_shape=out_shape,
        scratch_shapes=[pltpu.VMEM((k, tn), BF16)] if cast_w else [],
        compiler_params=_cparams(2))(*args)


def _ln_body(x_ref, f_ref, g_ref, b_ref, of_ref, ob_ref, *, alpha):
    y = alpha * x_ref[...] + f_ref[...]
    mu = jnp.mean(y, axis=-1, keepdims=True)
    yc = y - mu
    var = jnp.mean(yc * yc, axis=-1, keepdims=True)
    out = yc * lax.rsqrt(var + LN_EPS) * g_ref[...] + b_ref[...]
    of_ref[...] = out
    ob_ref[...] = out.astype(BF16)


def _deepnorm_ln(x, f, g, b, alpha, *, tm=256):
    m, d = x.shape
    tm = _tile(m, tm)
    row = pl.BlockSpec((tm, d), lambda i: (i, 0))
    vec = pl.BlockSpec((1, d), lambda i: (0, 0))
    return pl.pallas_call(
        functools.partial(_ln_body, alpha=alpha),
        grid=(m // tm,), in_specs=[row, row, vec, vec], out_specs=(row, row),
        out_shape=(jax.ShapeDtypeStruct((m, d), F32), jax.ShapeDtypeStruct((m, d), BF16)),
        compiler_params=_cparams(1))(x, f, g.reshape(1, d), b.reshape(1, d))


CONV_CH_CHUNK = 256
CONV_ROW_CHUNK = 32
A_PAD = 8
B_PAD = 32
LN_ROW_CHUNK = 16


def _convmix_body(gb_ref, gc_ref, h_ref, ga_ref, gg_ref, ctxa_ref, ctxb_ref, wa_ref, wb_ref,
                  bb_ref, ng_ref, nb_ref, mix_ref, sa_ref, sb_ref, exta, extb, cbuf, *, tt, d):
    t = pl.program_id(1)
    a0 = A_PAD - (CONV_A_W - 1)
    b0 = B_PAD - (CONV_B_W - 1)

    @pl.when(t == 0)
    def _():
        exta[a0:A_PAD, :] = ctxa_ref[0]
        extb[b0:B_PAD, :] = ctxb_ref[0]

    exta[A_PAD:A_PAD + tt, :] = gc_ref[0] * h_ref[0]
    extb[B_PAD:B_PAD + tt, :] = ga_ref[0] * jax.nn.sigmoid(gg_ref[0])

    cw = min(CONV_CH_CHUNK, d)
    rc = min(CONV_ROW_CHUNK, tt)

    def chunk(cc, carry):
        c0 = pl.multiple_of(cc * cw, cw)
        cols = pl.ds(c0, cw)
        for r0 in range(0, tt, rc):
            acc = wa_ref[0:1, cols] * exta[a0 + r0:a0 + r0 + rc, cols]
            for k in range(1, CONV_A_W):
                acc = acc + wa_ref[k:k + 1, cols] * exta[a0 + r0 + k:a0 + r0 + k + rc, cols]
            mix_ref[0, r0:r0 + rc, cols] = (gb_ref[0, r0:r0 + rc, cols] * acc).astype(BF16)
            accb = wb_ref[0:1, cols] * extb[b0 + r0:b0 + r0 + rc, cols]
            for k in range(1, CONV_B_W):
                accb = accb + wb_ref[k:k + 1, cols] * extb[b0 + r0 + k:b0 + r0 + k + rc, cols]
            cbuf[r0:r0 + rc, cols] = accb + bb_ref[0:1, cols]
        return carry

    lax.fori_loop(0, d // cw, chunk, 0)

    lr = min(LN_ROW_CHUNK, tt)

    def ln_rows(i, carry):
        r0 = pl.multiple_of(i * lr, lr)
        c = cbuf[pl.ds(r0, lr), :]
        mu = jnp.mean(c, axis=-1, keepdims=True)
        cc = c - mu
        var = jnp.mean(cc * cc, axis=-1, keepdims=True)
        y = cc * lax.rsqrt(var + LN_EPS) * ng_ref[...] + nb_ref[...]
        mix_ref[0, pl.ds(r0, lr), d:2 * d] = (y * jax.nn.sigmoid(y)).astype(BF16)
        return carry

    lax.fori_loop(0, tt // lr, ln_rows, 0)

    ta = exta[a0 + tt:A_PAD + tt, :]
    tb = extb[b0 + tt:B_PAD + tt, :]
    exta[a0:A_PAD, :] = ta
    extb[b0:B_PAD, :] = tb

    @pl.when(t == pl.num_programs(1) - 1)
    def _():
        sa_ref[0] = ta
        sb_ref[0] = tb


def _conv_mixer_prompt(proj, ctx_a, ctx_b, wa, wb, bb, ng, nb, *, tt=256):
    bsz, t, _ = proj.shape
    d = wa.shape[1]
    tt = _tile(t, tt)
    assert tt >= CONV_B_W - 1 and tt % min(CONV_ROW_CHUNK, tt) == 0 and d % min(CONV_CH_CHUNK, d) == 0
    col = lambda c: pl.BlockSpec((1, tt, d), lambda b, i, c=c: (b, i, c))
    full = lambda r: pl.BlockSpec((r, d), lambda b, i: (0, 0))
    st = lambda r: pl.BlockSpec((1, r, d), lambda b, i: (b, 0, 0))
    return pl.pallas_call(
        functools.partial(_convmix_body, tt=tt, d=d),
        grid=(bsz, t // tt),
        in_specs=[col(0), col(1), col(2), col(3), col(4), st(CONV_A_W - 1), st(CONV_B_W - 1),
                  full(CONV_A_W), full(CONV_B_W), full(1), full(1), full(1)],
        out_specs=(pl.BlockSpec((1, tt, 2 * d), lambda b, i: (b, i, 0)), st(CONV_A_W - 1), st(CONV_B_W - 1)),
        out_shape=(jax.ShapeDtypeStruct((bsz, t, 2 * d), BF16),
                   jax.ShapeDtypeStruct((bsz, CONV_A_W - 1, d), F32),
                   jax.ShapeDtypeStruct((bsz, CONV_B_W - 1, d), F32)),
        scratch_shapes=[pltpu.VMEM((A_PAD + tt, d), F32), pltpu.VMEM((B_PAD + tt, d), F32),
                        pltpu.VMEM((tt, d), F32)],
        compiler_params=_cparams(2),
    )(proj, proj, proj, proj, proj, ctx_a, ctx_b, wa, wb, bb.reshape(1, d), ng.reshape(1, d), nb.reshape(1, d))


def _convmix_dec_body(proj_ref, ctxa_ref, ctxb_ref, wa_ref, wb_ref, bb_ref, ng_ref, nb_ref,
                      mix_ref, sa_ref, sb_ref, *, d):
    gate_b = proj_ref[:, 0:d]
    u = proj_ref[:, d:2 * d] * proj_ref[:, 2 * d:3 * d]
    glu = proj_ref[:, 3 * d:4 * d] * jax.nn.sigmoid(proj_ref[:, 4 * d:5 * d])
    acc = wa_ref[CONV_A_W - 1:CONV_A_W, :] * u
    for k in range(CONV_A_W - 1):
        acc = acc + wa_ref[k:k + 1, :] * ctxa_ref[k]
    mix_ref[:, 0:d] = (gate_b * acc).astype(BF16)
    accb = wb_ref[CONV_B_W - 1:CONV_B_W, :] * glu + bb_ref[...]
    for k in range(CONV_B_W - 1):
        accb = accb + wb_ref[k:k + 1, :] * ctxb_ref[k]
    mu = jnp.mean(accb, axis=-1, keepdims=True)
    cc = accb - mu
    var = jnp.mean(cc * cc, axis=-1, keepdims=True)
    y = cc * lax.rsqrt(var + LN_EPS) * ng_ref[...] + nb_ref[...]
    mix_ref[:, d:2 * d] = (y * jax.nn.sigmoid(y)).astype(BF16)
    for k in range(CONV_A_W - 2):
        sa_ref[k] = ctxa_ref[k + 1]
    sa_ref[CONV_A_W - 2] = u
    for k in range(CONV_B_W - 2):
        sb_ref[k] = ctxb_ref[k + 1]
    sb_ref[CONV_B_W - 2] = glu


def _conv_mixer_decode(proj, ctx_a_t, ctx_b_t, wa, wb, bb, ng, nb):
    s = proj.shape[0]
    d = wa.shape[1]
    return pl.pallas_call(
        functools.partial(_convmix_dec_body, d=d),
        out_shape=(jax.ShapeDtypeStruct((s, 2 * d), BF16),
                   jax.ShapeDtypeStruct((CONV_A_W - 1, s, d), F32),
                   jax.ShapeDtypeStruct((CONV_B_W - 1, s, d), F32)),
        compiler_params=pltpu.CompilerParams(vmem_limit_bytes=VMEM_LIMIT_BYTES),
    )(proj, ctx_a_t, ctx_b_t, wa, wb, bb.reshape(1, d), ng.reshape(1, d), nb.reshape(1, d))


FFN_PAD = 8


def _gelu(x):
    return 0.5 * x * (1.0 + lax.erf(x * (1.0 / math.sqrt(2.0))))


FFN_ROW_BLOCK = 256


def _ffn_up_body(x_ref, wg_ref, wv_ref, cw_ref, cb_ref, ctx_ref, h_ref, st_ref, wgb, wvb, ext, vbuf, *, tm, rb):
    t = pl.program_id(2)
    e0 = FFN_PAD - (FFN_CONV_W - 1)

    @pl.when(jnp.logical_and(pl.program_id(1) == 0, t == 0))
    def _():
        wgb[...] = wg_ref[...].astype(BF16)
        wvb[...] = wv_ref[...].astype(BF16)

    @pl.when(t == 0)
    def _():
        ext[e0:FFN_PAD, :] = ctx_ref[0]

    def matmuls(r):
        x = x_ref[0, r * rb:(r + 1) * rb, :]
        ext[FFN_PAD + r * rb:FFN_PAD + (r + 1) * rb, :] = jnp.dot(x, wgb[...], preferred_element_type=F32)
        vbuf[r * rb:(r + 1) * rb, :] = jnp.dot(x, wvb[...], preferred_element_type=F32)

    rc = min(CONV_ROW_CHUNK, rb)

    def epilogue(r):
        for r0 in range(r * rb, (r + 1) * rb, rc):
            acc = cw_ref[0:1, :] * ext[e0 + r0:e0 + r0 + rc, :] + cb_ref[...]
            for k in range(1, FFN_CONV_W):
                acc = acc + cw_ref[k:k + 1, :] * ext[e0 + r0 + k:e0 + r0 + k + rc, :]
            h_ref[0, r0:r0 + rc, :] = (_gelu(acc) * vbuf[r0:r0 + rc, :]).astype(BF16)

    nsub = tm // rb
    for r in range(nsub + 1):
        if r < nsub:
            matmuls(r)
        if r > 0:
            epilogue(r - 1)

    tail = ext[e0 + tm:FFN_PAD + tm, :]
    ext[e0:FFN_PAD, :] = tail

    @pl.when(t == pl.num_programs(2) - 1)
    def _():
        st_ref[0] = tail


def _ffn_up_prompt(x, w_up, cw, cb, ctx, *, tm=1024, tn=256):
    bsz, t, k = x.shape
    f = cw.shape[1]
    tm = _tile(t, tm)
    rb = _tile(tm, FFN_ROW_BLOCK)
    tn = _tile(f, tn, LANES)
    nc = f // tn
    assert rb % min(CONV_ROW_CHUNK, rb) == 0
    return pl.pallas_call(
        functools.partial(_ffn_up_body, tm=tm, rb=rb),
        grid=(nc, bsz, t // tm),
        in_specs=[pl.BlockSpec((1, tm, k), lambda c, b, i: (b, i, 0)),
                  pl.BlockSpec((k, tn), lambda c, b, i: (0, c)),
                  pl.BlockSpec((k, tn), lambda c, b, i: (0, nc + c)),
                  pl.BlockSpec((FFN_CONV_W, tn), lambda c, b, i: (0, c)),
                  pl.BlockSpec((1, tn), lambda c, b, i: (0, c)),
                  pl.BlockSpec((1, FFN_CONV_W - 1, tn), lambda c, b, i: (b, 0, c))],
        out_specs=(pl.BlockSpec((1, tm, tn), lambda c, b, i: (b, i, c)),
                   pl.BlockSpec((1, FFN_CONV_W - 1, tn), lambda c, b, i: (b, 0, c))),
        out_shape=(jax.ShapeDtypeStruct((bsz, t, f), BF16),
                   jax.ShapeDtypeStruct((bsz, FFN_CONV_W - 1, f), F32)),
        scratch_shapes=[pltpu.VMEM((k, tn), BF16), pltpu.VMEM((k, tn), BF16),
                        pltpu.VMEM((FFN_PAD + tm, tn), F32), pltpu.VMEM((tm, tn), F32)],
        compiler_params=_cparams(3),
    )(x, w_up, w_up, cw, cb.reshape(1, f), ctx)


def _ffn_up_dec_body(x_ref, wg_ref, wv_ref, cw_ref, cb_ref, ctx_ref, h_ref, st_ref):
    x = x_ref[...]
    g = jnp.dot(x, wg_ref[...].astype(BF16), preferred_element_type=F32)
    v = jnp.dot(x, wv_ref[...].astype(BF16), preferred_element_type=F32)
    acc = cw_ref[FFN_CONV_W - 1:FFN_CONV_W, :] * g + cb_ref[...]
    for k in range(FFN_CONV_W - 1):
        acc = acc + cw_ref[k:k + 1, :] * ctx_ref[k]
    h_ref[...] = (_gelu(acc) * v).astype(BF16)
    for k in range(FFN_CONV_W - 2):
        st_ref[k] = ctx_ref[k + 1]
    st_ref[FFN_CONV_W - 2] = g


def _ffn_up_decode(x, w_up, cw, cb, ctx_t, *, tn=512):
    s, k = x.shape
    f = cw.shape[1]
    tn = _tile(f, tn, LANES)
    nc = f // tn
    return pl.pallas_call(
        _ffn_up_dec_body,
        grid=(nc,),
        in_specs=[pl.BlockSpec((s, k), lambda c: (0, 0)),
                  pl.BlockSpec((k, tn), lambda c: (0, c)),
                  pl.BlockSpec((k, tn), lambda c: (0, nc + c)),
                  pl.BlockSpec((FFN_CONV_W, tn), lambda c: (0, c)),
                  pl.BlockSpec((1, tn), lambda c: (0, c)),
                  pl.BlockSpec((FFN_CONV_W - 1, s, tn), lambda c: (0, 0, c))],
        out_specs=(pl.BlockSpec((s, tn), lambda c: (0, c)),
                   pl.BlockSpec((FFN_CONV_W - 1, s, tn), lambda c: (0, 0, c))),
        out_shape=(jax.ShapeDtypeStruct((s, f), BF16),
                   jax.ShapeDtypeStruct((FFN_CONV_W - 1, s, f), F32)),
        compiler_params=_cparams(1),
    )(x, w_up, w_up, cw, cb.reshape(1, f), ctx_t)


def _qk(q, k):
    return lax.dot_general(q, k, (((1,), (1,)), ((), ())), preferred_element_type=F32)


def _softplus(z):
    return jnp.maximum(z, 0.0) + jnp.log(1.0 + jnp.exp(-jnp.abs(z)))


def _suffix_sums(lk, tri):
    hi = lk.astype(BF16)
    lo = (lk - hi.astype(F32)).astype(BF16)
    return (jnp.dot(hi, tri, preferred_element_type=F32) + jnp.dot(lo, tri, preferred_element_type=F32))


def _strict_lower_ones(n):
    r = lax.broadcasted_iota(jnp.int32, (n, n), 0)
    c = lax.broadcasted_iota(jnp.int32, (n, n), 1)
    return jnp.where(r > c, 1.0, 0.0).astype(BF16)


def _lambda(lq1_ref, lk1_ref, lq2_ref, lk2_ref, lambda_init):
    s1 = jnp.sum(lq1_ref[...] * lk1_ref[...], axis=-1, keepdims=True)
    s2 = jnp.sum(lq2_ref[...] * lk2_ref[...], axis=-1, keepdims=True)
    return jnp.exp(s1) - jnp.exp(s2) + lambda_init


def _sb_prompt_body(q_ref, k_ref, v_ref, o_ref, acc_ref, c_ref, *, tq, hp, scale):
    qi = pl.program_id(2)
    tri = _strict_lower_ones(tq)
    row = lax.broadcasted_iota(jnp.int32, (tq, tq), 0)
    col = lax.broadcasted_iota(jnp.int32, (tq, tq), 1)

    def block(kb, diagonal):
        ks = pl.ds(pl.multiple_of(kb * tq, tq), tq)
        heads = range(hp)
        cols = [slice(hh * HD, (hh + 1) * HD) for hh in heads]
        z = [_qk(q_ref[0, :, cols[hh]], k_ref[0, ks, cols[hh]]) * scale for hh in heads]
        sp = [_softplus(z[hh]) for hh in heads]
        lk = [jnp.where(col < row, -sp[hh], 0.0) if diagonal else -sp[hh] for hh in heads]
        within = [_suffix_sums(lk[hh], tri) for hh in heads]
        block_sum = [within[hh][:, 0:1] + lk[hh][:, 0:1] for hh in heads]
        if diagonal:
            a = [jnp.where(col < row, jnp.exp(z[hh] - sp[hh] + within[hh]), 0.0) for hh in heads]
            c_new = block_sum
        else:
            c_old = [c_ref[hh] for hh in heads]
            a = [jnp.exp(z[hh] - sp[hh] + (c_old[hh] + within[hh])) for hh in heads]
            c_new = [c_old[hh] + block_sum[hh] for hh in heads]
        pv = [jnp.dot(a[hh].astype(BF16), v_ref[0, ks, cols[hh]], preferred_element_type=F32) for hh in heads]
        if not diagonal:
            pv = [acc_ref[hh] + pv[hh] for hh in heads]
        for hh in heads:
            c_ref[hh] = c_new[hh]
            acc_ref[hh] = pv[hh]

    block(qi, True)

    def body(i, carry):
        block(qi - 1 - i, False)
        return carry

    lax.fori_loop(0, qi, body, 0)
    for hh in range(hp):
        o_ref[0, :, hh * HD:(hh + 1) * HD] = acc_ref[hh].astype(o_ref.dtype)


def _sb_attention_prompt(q, k, v, *, tq=256, hp=4):
    bsz, t, d = q.shape
    n_heads = d // HD
    tq = _tile(t, tq)
    hp = _tile(n_heads, hp, 1)
    w = hp * HD
    qspec = pl.BlockSpec((1, tq, w), lambda b, h, i: (b, i, h))
    kvspec = pl.BlockSpec((1, t, w), lambda b, h, i: (b, 0, h))
    return pl.pallas_call(
        functools.partial(_sb_prompt_body, tq=tq, hp=hp, scale=HD ** -0.5),
        grid=(bsz, n_heads // hp, t // tq),
        in_specs=[qspec, kvspec, kvspec], out_specs=qspec,
        out_shape=jax.ShapeDtypeStruct((bsz, t, d), BF16),
        scratch_shapes=[pltpu.VMEM((hp, tq, HD), F32), pltpu.VMEM((hp, tq, 1), F32)],
        compiler_params=_cparams(3),
    )(q, k, v)


def _df_prompt_body(q_ref, k_ref, v_ref, lq1_ref, lk1_ref, lq2_ref, lk2_ref, g_ref, o_ref,
                    m_ref, l_ref, acc_ref, *, tq, hp, scale, lambda_init):
    qi = pl.program_id(2)
    row = lax.broadcasted_iota(jnp.int32, (tq, tq), 0)
    col = lax.broadcasted_iota(jnp.int32, (tq, tq), 1)
    w = 2 * HD

    def block(kb, diagonal):
        ks = pl.ds(pl.multiple_of(kb * tq, tq), tq)
        maps = range(2 * hp)
        cols = [slice(n * HD, (n + 1) * HD) for n in maps]
        s = [_qk(q_ref[0, :, cols[n]], k_ref[0, ks, cols[n]]) * scale for n in maps]
        if diagonal:
            s = [jnp.where(col <= row, s[n], -jnp.inf) for n in maps]
            m_new = [jnp.max(s[n], axis=-1, keepdims=True) for n in maps]
        else:
            m_old = [m_ref[n] for n in maps]
            m_new = [jnp.maximum(m_old[n], jnp.max(s[n], axis=-1, keepdims=True)) for n in maps]
            alpha = [jnp.exp(m_old[n] - m_new[n]) for n in maps]
        p = [jnp.exp(s[n] - m_new[n]) for n in maps]
        l_new = [jnp.sum(p[n], axis=-1, keepdims=True) for n in maps]
        pv = [jnp.dot(p[n].astype(BF16), v_ref[0, ks, (n // 2) * w:(n // 2 + 1) * w], preferred_element_type=F32)
              for n in maps]
        if not diagonal:
            l_new = [alpha[n] * l_ref[n] + l_new[n] for n in maps]
            pv = [alpha[n] * acc_ref[n] + pv[n] for n in maps]
        for n in maps:
            m_ref[n] = m_new[n]
            l_ref[n] = l_new[n]
            acc_ref[n] = pv[n]

    block(qi, True)

    def body(i, carry):
        block(qi - 1 - i, False)
        return carry

    lax.fori_loop(0, qi, body, 0)

    lam = _lambda(lq1_ref, lk1_ref, lq2_ref, lk2_ref, lambda_init)
    for hh in range(hp):
        o = acc_ref[2 * hh] / l_ref[2 * hh] - lam * (acc_ref[2 * hh + 1] / l_ref[2 * hh + 1])
        y = o * lax.rsqrt(jnp.mean(o * o, axis=-1, keepdims=True) + RMS_EPS) * g_ref[...]
        o_ref[0, :, hh * w:(hh + 1) * w] = (y * (1.0 - lambda_init)).astype(o_ref.dtype)


def _df_attention_prompt(q, k, v, lam_params, subln_g, lambda_init, *, tq=256, hp=2):
    bsz, t, d = q.shape
    w = 2 * HD
    n_heads = d // w
    tq = _tile(t, tq)
    hp = _tile(n_heads, hp, 1)
    wb = hp * w
    qspec = pl.BlockSpec((1, tq, wb), lambda b, h, i: (b, i, h))
    kvspec = pl.BlockSpec((1, t, wb), lambda b, h, i: (b, 0, h))
    vec = pl.BlockSpec((1, HD), lambda b, h, i: (0, 0))
    return pl.pallas_call(
        functools.partial(_df_prompt_body, tq=tq, hp=hp, scale=HD ** -0.5, lambda_init=lambda_init),
        grid=(bsz, n_heads // hp, t // tq),
        in_specs=[qspec, kvspec, kvspec, vec, vec, vec, vec, pl.BlockSpec((1, w), lambda b, h, i: (0, 0))],
        out_specs=qspec,
        out_shape=jax.ShapeDtypeStruct((bsz, t, d), BF16),
        scratch_shapes=[pltpu.VMEM((2 * hp, tq, 1), F32), pltpu.VMEM((2 * hp, tq, 1), F32),
                        pltpu.VMEM((2 * hp, tq, w), F32)],
        compiler_params=_cparams(3),
    )(q, k, v, *[p.reshape(1, HD) for p in lam_params], subln_g.reshape(1, w))


DECODE_PAGES_PER_STEP = 4


def _own_head_rows(s, n_heads, tiles, diag):
    return [jnp.sum(jnp.where(diag, s[:, j * LANES:(j + 1) * LANES], 0.0), axis=0, keepdims=True)
            for j in range(tiles)]


def _spread_rows(a, r0, n_heads, tiles, diag):
    return jnp.concatenate(
        [jnp.where(diag, jnp.broadcast_to(a[r0 + j:r0 + j + 1, :], (n_heads, LANES)), 0.0) for j in range(tiles)],
        axis=1)


def _head_diag(n_heads):
    sub = lax.broadcasted_iota(jnp.int32, (n_heads, LANES), 0)
    lane = lax.broadcasted_iota(jnp.int32, (n_heads, LANES), 1)
    return lane % n_heads == sub


def _split_dot_rhs(lhs, rhs):
    hi = rhs.astype(BF16)
    lo = (rhs - hi.astype(F32)).astype(BF16)
    return jnp.dot(lhs, hi, preferred_element_type=F32) + jnp.dot(lhs, lo, preferred_element_type=F32)


def _sb_decode_body(pt_ref, q_ref, *refs, n_heads, page, group, scale):
    k_refs = refs[:group]
    v_refs = refs[group:2 * group]
    o_ref, acc_ref, c_ref = refs[2 * group:]
    p = pl.program_id(1)
    tiles = page * n_heads // LANES
    nrow = group * tiles

    @pl.when(p == 0)
    def _():
        acc_ref[...] = jnp.zeros_like(acc_ref)
        c_ref[...] = jnp.zeros_like(c_ref)

    q = q_ref[0]
    diag = _head_diag(n_heads)
    zrows = []
    for g in range(group):
        zrows += _own_head_rows(_qk(q, k_refs[g][...].astype(BF16)), n_heads, tiles, diag)
    z = jnp.concatenate(zrows, axis=0) * scale
    sp = _softplus(z)
    lk = -sp

    li = lax.broadcasted_iota(jnp.int32, (LANES, LANES), 0)
    lj = lax.broadcasted_iota(jnp.int32, (LANES, LANES), 1)
    same_head = li % n_heads == lj % n_heads
    later_in_row = jnp.where(jnp.logical_and(same_head, li // n_heads > lj // n_heads), 1.0, 0.0).astype(BF16)
    whole_row = jnp.where(same_head, 1.0, 0.0).astype(BF16)
    within = _suffix_sums(lk, later_in_row)
    row_tot = _suffix_sums(lk, whole_row)
    ri = lax.broadcasted_iota(jnp.int32, (nrow, nrow), 0)
    rj = lax.broadcasted_iota(jnp.int32, (nrow, nrow), 1)
    later_rows = _split_dot_rhs(jnp.where(rj > ri, 1.0, 0.0).astype(BF16), row_tot)
    a = jnp.exp(z - sp + (c_ref[...] + later_rows + within))
    c_ref[...] = c_ref[...] + later_rows[0:1, :] + row_tot[0:1, :]

    upd = jnp.zeros((n_heads, HD), F32)
    for g in range(group):
        a_sp = _spread_rows(a, g * tiles, n_heads, tiles, diag).astype(BF16)
        upd = upd + jnp.dot(a_sp, v_refs[g][...].astype(BF16), preferred_element_type=F32)
    acc_ref[...] = acc_ref[...] + upd

    @pl.when(p == pl.num_programs(1) - 1)
    def _():
        o_ref[0] = acc_ref[...].astype(o_ref.dtype)


def _page_specs(block, page_table_cols, group, layer):
    def spec(g):
        return pl.BlockSpec(block, lambda b, p, pt: (pt[b, page_table_cols - (p + 1) * group + g], layer, 0, 0))
    return [spec(g) for g in range(group)]


def _sb_attention_decode(q, cache_k, cache_v, page_table, layer):
    s, n_heads, _ = q.shape
    page = cache_k.shape[2] // n_heads
    n_pages = page_table.shape[1]
    group = _tile(n_pages, DECODE_PAGES_PER_STEP, 1)
    assert (page * n_heads) % LANES == 0 and LANES % n_heads == 0
    kv = _page_specs((None, None, page * n_heads, HD), n_pages, group, layer)
    hd = pl.BlockSpec((1, n_heads, HD), lambda b, p, pt: (b, 0, 0))
    return pl.pallas_call(
        functools.partial(_sb_decode_body, n_heads=n_heads, page=page, group=group, scale=HD ** -0.5),
        grid_spec=pltpu.PrefetchScalarGridSpec(
            num_scalar_prefetch=1, grid=(s, n_pages // group), in_specs=[hd] + kv + kv, out_specs=hd,
            scratch_shapes=[pltpu.VMEM((n_heads, HD), F32), pltpu.VMEM((1, LANES), F32)]),
        out_shape=jax.ShapeDtypeStruct((s, n_heads, HD), BF16),
        compiler_params=_cparams(2),
    )(page_table, q, *([cache_k] * group), *([cache_v] * group))


def _head_lanes_max(x, n_heads):
    sh = n_heads
    while sh < LANES:
        x = jnp.maximum(x, pltpu.roll(x, sh, 1))
        sh *= 2
    return x


def _head_lanes_sum(x, n_heads):
    sh = n_heads
    while sh < LANES:
        x = x + pltpu.roll(x, sh, 1)
        sh *= 2
    return x


def _lanes_to_rows(x, n_heads, diag):
    lane = lax.broadcasted_iota(jnp.int32, (n_heads, LANES), 1)
    pick = jnp.logical_and(diag, lane < n_heads)
    return jnp.sum(jnp.where(pick, jnp.broadcast_to(x, (n_heads, LANES)), 0.0), axis=1, keepdims=True)


def _df_decode_body(pt_ref, q_ref, kn_ref, vn_ref, *refs, n_heads, page, group, scale, lambda_init):
    k_refs = refs[:group]
    v_refs = refs[group:2 * group]
    lq1_ref, lk1_ref, lq2_ref, lk2_ref, g_ref, o_ref, m_ref, l_ref, acc_ref = refs[2 * group:]
    p = pl.program_id(1)
    tiles = page * n_heads // LANES
    rows_per_map = page * n_heads

    @pl.when(p == 0)
    def _():
        m_ref[...] = jnp.full_like(m_ref, -jnp.inf)
        l_ref[...] = jnp.zeros_like(l_ref)
        acc_ref[...] = jnp.zeros_like(acc_ref)

    diag = _head_diag(n_heads)
    values = [v_refs[g][...].astype(BF16) for g in range(group)]
    srows = [[], []]
    for c in range(2):
        qc = q_ref[0, c * n_heads:(c + 1) * n_heads, :]
        for g in range(group):
            kc = k_refs[g][pl.ds(c, rows_per_map, stride=2), :].astype(BF16)
            srows[c] += _own_head_rows(_qk(qc, kc), n_heads, tiles, diag)
    s = [jnp.concatenate(srows[c], axis=0) * scale for c in range(2)]
    m_old = [m_ref[c] for c in range(2)]
    m_new = [jnp.maximum(m_old[c], _head_lanes_max(jnp.max(s[c], axis=0, keepdims=True), n_heads))
             for c in range(2)]
    alpha = [jnp.exp(m_old[c] - m_new[c]) for c in range(2)]
    pr = [jnp.exp(s[c] - m_new[c]) for c in range(2)]
    l_new = [alpha[c] * l_ref[c] + jnp.sum(pr[c], axis=0, keepdims=True) for c in range(2)]
    upd = [jnp.zeros((n_heads, 2 * HD), F32) for c in range(2)]
    for g in range(group):
        for c in range(2):
            p_sp = _spread_rows(pr[c], g * tiles, n_heads, tiles, diag).astype(BF16)
            upd[c] = upd[c] + jnp.dot(p_sp, values[g], preferred_element_type=F32)
    acc_new = [_lanes_to_rows(alpha[c], n_heads, diag) * acc_ref[c] + upd[c] for c in range(2)]
    for c in range(2):
        m_ref[c] = m_new[c]
        l_ref[c] = l_new[c]
        acc_ref[c] = acc_new[c]

    @pl.when(p == pl.num_programs(1) - 1)
    def _():
        outs = []
        for c in range(2):
            rows = slice(c * n_heads, (c + 1) * n_heads)
            m_col = _lanes_to_rows(m_ref[c], n_heads, diag)
            l_col = _lanes_to_rows(_head_lanes_sum(l_ref[c], n_heads), n_heads, diag)
            s_new = jnp.sum(q_ref[0, rows, :].astype(F32) * kn_ref[0, rows, :], axis=-1, keepdims=True) * scale
            m_fin = jnp.maximum(m_col, s_new)
            a_fin = jnp.exp(m_col - m_fin)
            p_new = jnp.exp(s_new - m_fin)
            outs.append((a_fin * acc_ref[c] + p_new * vn_ref[0]) / (a_fin * l_col + p_new))
        lam = _lambda(lq1_ref, lk1_ref, lq2_ref, lk2_ref, lambda_init)
        o = outs[0] - lam * outs[1]
        y = o * lax.rsqrt(jnp.mean(o * o, axis=-1, keepdims=True) + RMS_EPS) * g_ref[...]
        o_ref[0] = (y * (1.0 - lambda_init)).astype(o_ref.dtype)


def _df_attention_decode(q, k_new, v_new, cache_k, cache_v, page_table, layer, lam_params, subln_g, lambda_init):
    s, nr, _ = q.shape
    n_heads = nr // 2
    page = cache_v.shape[2] // n_heads
    n_pages = page_table.shape[1]
    group = _tile(n_pages, DECODE_PAGES_PER_STEP, 1)
    assert (page * n_heads) % LANES == 0 and LANES % n_heads == 0
    w = 2 * HD
    kspecs = _page_specs((None, None, page * nr, HD), n_pages, group, layer)
    vspecs = _page_specs((None, None, page * n_heads, w), n_pages, group, layer)
    qspec = pl.BlockSpec((1, nr, HD), lambda b, p, pt: (b, 0, 0))
    vnspec = pl.BlockSpec((1, n_heads, w), lambda b, p, pt: (b, 0, 0))
    vec = pl.BlockSpec((1, HD), lambda b, p, pt: (0, 0))
    return pl.pallas_call(
        functools.partial(_df_decode_body, n_heads=n_heads, page=page, group=group, scale=HD ** -0.5,
                          lambda_init=lambda_init),
        grid_spec=pltpu.PrefetchScalarGridSpec(
            num_scalar_prefetch=1, grid=(s, n_pages // group),
            in_specs=[qspec, qspec, vnspec] + kspecs + vspecs + [vec, vec, vec, vec,
                                                                 pl.BlockSpec((1, w), lambda b, p, pt: (0, 0))],
            out_specs=vnspec,
            scratch_shapes=[pltpu.VMEM((2, 1, LANES), F32), pltpu.VMEM((2, 1, LANES), F32),
                            pltpu.VMEM((2, n_heads, w), F32)]),
        out_shape=jax.ShapeDtypeStruct((s, n_heads, w), BF16),
        compiler_params=_cparams(2),
    )(page_table, q, k_new, v_new, *([cache_k] * group), *([cache_v] * group),
      *[p.reshape(1, HD) for p in lam_params], subln_g.reshape(1, w))


def _rope_tables(positions):
    half = HD // 2
    inv_freq = ROPE_THETA ** (-jnp.arange(half, dtype=F32) / half)
    ang = positions.astype(F32)[:, None] * inv_freq[None, :]
    cos, sin = jnp.cos(ang), jnp.sin(ang)
    return jnp.concatenate([cos, cos], axis=-1), jnp.concatenate([-sin, sin], axis=-1)


def kernel(x_prompt, x_sample, state_conv_a, state_conv_b, state_ffn, cache_sb_k, cache_sb_v, cache_df_k, cache_df_v, page_table, w_in_conv, conv_a_w, conv_b_w, conv_b_b, norm_b_g, norm_b_b, w_out_conv, w_in_attn, lambda_q1, lambda_k1, lambda_q2, lambda_k2, subln_g, w_out_attn, ln1_g, ln1_b, ln2_g, ln2_b, w_ffn_up, ffn_conv_w, ffn_conv_b, w_ffn_down):
    bp, t, dm = x_prompt.shape
    bs = x_sample.shape[0]
    assert x_sample.shape[1] == 1
    depth = ln1_g.shape[0]
    d_a = conv_a_w.shape[2]
    d_ff = ffn_conv_w.shape[2]
    h_sb = cache_sb_k.shape[3]
    h_df = cache_df_v.shape[3]
    d_sb = h_sb * HD
    d_df = h_df * 2 * HD
    page = cache_sb_k.shape[2]
    past_len = page_table.shape[1] * page
    alpha = (2 * depth) ** 0.25
    mp = bp * t

    hp = x_prompt.reshape(mp, dm)
    hs = x_sample.reshape(bs, dm)
    hp_b = hp.astype(BF16)
    hs_b = hs.astype(BF16)

    ca_p, ca_s, cb_p, cb_s, f_p, f_s = [], [], [], [], [], []
    rows_p = ([], [], [], [])
    rows_s = ([], [], [], [])

    for layer in range(depth):
        i = layer // 2
        if layer % 2 == 0:
            w_in, w_out = w_in_conv[i], w_out_conv[i]
            cw = (conv_a_w[i], conv_b_w[i], conv_b_b[i], norm_b_g[i], norm_b_b[i])
            (proj_p,) = _matmul([hp_b], w_in, (F32,), tm=PROJ_TM, tn=PROJ_TN)
            mixed_p, sa, sb = _conv_mixer_prompt(
                proj_p.reshape(bp, t, 5 * d_a), jnp.zeros((bp, CONV_A_W - 1, d_a), F32),
                jnp.zeros((bp, CONV_B_W - 1, d_a), F32), *cw)
            ca_p.append(sa)
            cb_p.append(sb)
            (m_p,) = _matmul([mixed_p.reshape(mp, 2 * d_a)], w_out, (F32,), tm=PROJ_TM, tn=PROJ_TN)

            (proj_s,) = _matmul([hs_b], w_in, (F32,), tm=bs, tn=PROJ_TN)
            mixed_s, sa, sb = _conv_mixer_decode(
                proj_s, jnp.swapaxes(state_conv_a[i], 0, 1), jnp.swapaxes(state_conv_b[i], 0, 1), *cw)
            ca_s.append(jnp.swapaxes(sa, 0, 1))
            cb_s.append(jnp.swapaxes(sb, 0, 1))
            (m_s,) = _matmul([mixed_s], w_out, (F32,), tm=bs, tn=PROJ_TN)
        else:
            lambda_init = 0.8 - 0.6 * math.exp(-0.3 * layer)
            w_in, w_out = w_in_attn[i], w_out_attn[i]
            lam_params = (lambda_q1[i], lambda_k1[i], lambda_q2[i], lambda_k2[i])
            edges = (0, d_sb, 2 * d_sb, 3 * d_sb, 3 * d_sb + d_df, 3 * d_sb + 2 * d_df, 3 * d_sb + 3 * d_df)
            span = lambda n: (edges[n], edges[n + 1])

            def qkv_proj(x_b, tm, rope):
                mm = functools.partial(_matmul, [x_b], w_in, tm=tm, tn=PROJ_TN)
                (q_sb,) = mm((BF16,), cols=span(0))
                k_sb = mm((F32, BF16), cols=span(1))
                v_sb = mm((F32, BF16), cols=span(2))
                (q_df,) = mm((BF16,), cols=span(3), rope=rope)
                k_df = mm((F32, BF16), cols=span(4), rope=rope)
                v_df = mm((F32, BF16), cols=span(5))
                return q_sb, k_sb, v_sb, q_df, k_df, v_df

            cos_p, sin_p = _rope_tables(jnp.arange(t, dtype=jnp.int32))
            q_sb, k_sb, v_sb, q_df, k_df, v_df = qkv_proj(hp_b, PROJ_TM, (cos_p, sin_p, t))
            new_p = (k_sb[0].reshape(bp, t, h_sb, HD), v_sb[0].reshape(bp, t, h_sb, HD),
                     k_df[0].reshape(bp, t, h_df, 2, HD), v_df[0].reshape(bp, t, h_df, 2 * HD))
            b3 = lambda a: a.reshape(bp, t, -1)
            o_sb = _sb_attention_prompt(b3(q_sb), b3(k_sb[1]), b3(v_sb[1]))
            o_df = _df_attention_prompt(b3(q_df), b3(k_df[1]), b3(v_df[1]), lam_params, subln_g[i], lambda_init)
            (m_p,) = _matmul([o_sb.reshape(mp, d_sb), o_df.reshape(mp, d_df)], w_out, (F32,),
                             tm=PROJ_TM, tn=PROJ_TN)

            cos_s, sin_s = _rope_tables(jnp.full((1,), past_len, jnp.int32))
            q_sb, k_sb, v_sb, q_df, k_df, v_df = qkv_proj(hs_b, bs, (cos_s, sin_s, 1))
            new_s = (k_sb[0].reshape(bs, 1, h_sb, HD), v_sb[0].reshape(bs, 1, h_sb, HD),
                     k_df[0].reshape(bs, 1, h_df, 2, HD), v_df[0].reshape(bs, 1, h_df, 2 * HD))
            pool, n_layers = cache_sb_k.shape[:2]
            o_sb_s = _sb_attention_decode(
                q_sb.reshape(bs, h_sb, HD), cache_sb_k.reshape(pool, n_layers, page * h_sb, HD),
                cache_sb_v.reshape(pool, n_layers, page * h_sb, HD), page_table, i)
            q_dec = jnp.swapaxes(q_df.reshape(bs, h_df, 2, HD), 1, 2).reshape(bs, 2 * h_df, HD)
            k_dec = jnp.swapaxes(k_df[0].reshape(bs, h_df, 2, HD), 1, 2).reshape(bs, 2 * h_df, HD)
            o_df_s = _df_attention_decode(
                q_dec, k_dec, v_df[0].reshape(bs, h_df, 2 * HD),
                cache_df_k.reshape(pool, n_layers, page * h_df * 2, HD),
                cache_df_v.reshape(pool, n_layers, page * h_df, 2 * HD), page_table, i,
                lam_params, subln_g[i], lambda_init)
            (m_s,) = _matmul([o_sb_s.reshape(bs, d_sb), o_df_s.reshape(bs, d_df)], w_out, (F32,),
                             tm=bs, tn=PROJ_TN)
            for lst, r in zip(rows_p, new_p):
                lst.append(r)
            for lst, r in zip(rows_s, new_s):
                lst.append(r)

        hp, hp_b = _deepnorm_ln(hp, m_p, ln1_g[layer], ln1_b[layer], alpha)
        hs, hs_b = _deepnorm_ln(hs, m_s, ln1_g[layer], ln1_b[layer], alpha)

        w_up = w_ffn_up[layer]
        w_down = w_ffn_down[layer].astype(BF16)
        h_p, st_p = _ffn_up_prompt(hp_b.reshape(bp, t, dm), w_up, ffn_conv_w[layer], ffn_conv_b[layer],
                                   jnp.zeros((bp, FFN_CONV_W - 1, d_ff), F32))
        f_p.append(st_p)
        (fp,) = _matmul([h_p.reshape(mp, d_ff)], w_down, (F32,), tm=512, tn=512)
        h_s, st_s = _ffn_up_decode(hs_b, w_up, ffn_conv_w[layer], ffn_conv_b[layer],
                                   jnp.swapaxes(state_ffn[layer], 0, 1))
        f_s.append(jnp.swapaxes(st_s, 0, 1))
        (fs,) = _matmul([h_s], w_down, (F32,), tm=bs, tn=512)

        hp, hp_b = _deepnorm_ln(hp, fp, ln2_g[layer], ln2_b[layer], alpha)
        hs, hs_b = _deepnorm_ln(hs, fs, ln2_g[layer], ln2_b[layer], alpha)

    return (hp.reshape(bp, t, dm), hs.reshape(bs, 1, dm),
            jnp.stack(ca_p), jnp.stack(ca_s), jnp.stack(cb_p), jnp.stack(cb_s),
            jnp.stack(f_p), jnp.stack(f_s),
            jnp.stack(rows_p[0], axis=1), jnp.stack(rows_s[0], axis=1),
            jnp.stack(rows_p[1], axis=1), jnp.stack(rows_s[1], axis=1),
            jnp.stack(rows_p[2], axis=1), jnp.stack(rows_s[2], axis=1),
            jnp.stack(rows_p[3], axis=1), jnp.stack(rows_s[3], axis=1))
```

```python
import functools
import math

import jax
import jax.numpy as jnp
from jax import lax
from jax.experimental import pallas as pl
from jax.experimental.pallas import tpu as pltpu

F32 = jnp.float32
BF16 = jnp.bfloat16

HD = 128
CONV_A_W = 3
CONV_B_W = 31
FFN_CONV_W = 3
ROPE_THETA = 10000.0
LN_EPS = 1e-5
RMS_EPS = 1e-5

LANES = 128
SUBLANES = 8
VMEM_LIMIT_BYTES = 56 * 1024 * 1024


def _cparams(n_grid):
    return pltpu.CompilerParams(dimension_semantics=("arbitrary",) * n_grid,
                                vmem_limit_bytes=VMEM_LIMIT_BYTES)


def _tile(dim, pref, align=SUBLANES):
    if dim <= pref:
        return dim
    t = pref - pref % align
    while t >= align:
        if dim % t == 0:
            return t
        t -= align
    return dim


PROJ_TM = 1024
PROJ_TN = 512
DEC_TN = 1024


def _mm_body(*refs, n_x, n_out, rope, cast_w, keep_w, tn):
    x_refs = refs[:n_x]
    w_ref = refs[n_x]
    pos = n_x + 1
    if rope:
        cos_ref, sin_ref = refs[pos:pos + 2]
        pos += 2
    outs = refs[pos:pos + n_out]
    if cast_w:
        wb_ref = refs[-1]
        wf_ref = w_ref

        @pl.when(pl.program_id(1) == 0)
        def _():
            wb_ref[...] = wf_ref[...].astype(BF16)
            if keep_w:
                refs[pos + n_out][...] = wb_ref[...]

        w_ref = wb_ref
    acc = None
    k0 = 0
    for x_ref in x_refs:
        kk = x_ref.shape[1]
        part = jnp.dot(x_ref[...], w_ref[k0:k0 + kk, :], preferred_element_type=F32)
        acc = part if acc is None else acc + part
        k0 += kk
    if not rope:
        for o in outs:
            o[...] = acc.astype(o.dtype)
        return
    cos = cos_ref[...]
    sin = sin_ref[...]
    for c in range(tn // HD):
        xc = acc[:, c * HD:(c + 1) * HD]
        rc = xc * cos + pltpu.roll(xc, HD // 2, 1) * sin
        for o in outs:
            o[:, c * HD:(c + 1) * HD] = rc.astype(o.dtype)


def _matmul(xs, w, out_dtypes, *, tm, tn, cols=None, rope=None, layer=None, keep_w=False):
    xs = tuple(xs)
    m = xs[0].shape[0]
    k = sum(x.shape[1] for x in xs)
    assert k == w.shape[-2] and (layer is None) == (w.ndim == 2)
    lo, hi = cols if cols is not None else (0, w.shape[-1])
    n = hi - lo
    tm = _tile(m, tm)
    tn = _tile(math.gcd(n, lo) if lo else n, tn, LANES)
    assert lo % tn == 0 and n % tn == 0
    j0 = lo // tn
    cast_w = w.dtype != BF16
    assert cast_w or not keep_w
    in_specs = [pl.BlockSpec((tm, x.shape[1]), lambda j, i: (i, 0)) for x in xs]
    if layer is None:
        in_specs.append(pl.BlockSpec((k, tn), lambda j, i: (0, j0 + j)))
    else:
        in_specs.append(pl.BlockSpec((None, k, tn), lambda j, i: (layer, 0, j0 + j)))
    args = list(xs) + [w]
    if rope is not None:
        cos, sin, period = rope
        assert tn % HD == 0
        if period <= tm:
            assert tm % period == 0
            cos = jnp.tile(cos, (tm // period, 1))
            sin = jnp.tile(sin, (tm // period, 1))
            tspec = pl.BlockSpec((tm, HD), lambda j, i: (0, 0))
        else:
            assert period % tm == 0
            nper = period // tm
            tspec = pl.BlockSpec((tm, HD), lambda j, i: (i % nper, 0))
        in_specs += [tspec, tspec]
        args += [cos, sin]
    out_shape = tuple(jax.ShapeDtypeStruct((m, n), dt) for dt in out_dtypes)
    out_specs = tuple(pl.BlockSpec((tm, tn), lambda j, i: (i, j)) for _ in out_dtypes)
    if keep_w:
        out_shape += (jax.ShapeDtypeStruct((k, n), BF16),)
        out_specs += (pl.BlockSpec((k, tn), lambda j, i: (0, j)),)
    return pl.pallas_call(
        functools.partial(_mm_body, n_x=len(xs), n_out=len(out_dtypes), rope=rope is not None,
                          cast_w=cast_w, keep_w=keep_w, tn=tn),
        grid=(n // tn, m // tm), in_specs=in_specs, out_specs=out_specs, out_shape=out_shape,
        scratch_shapes=[pltpu.VMEM((k, tn), BF16)] if cast_w else [],
        compiler_params=_cparams(2))(*args)


def _ln_body(x_ref, f_ref, g_ref, b_ref, of_ref, ob_ref, *, alpha):
    y = alpha * x_ref[...] + f_ref[...]
    mu = jnp.mean(y, axis=-1, keepdims=True)
    yc = y - mu
    var = jnp.mean(yc * yc, axis=-1, keepdims=True)
    out = yc * lax.rsqrt(var + LN_EPS) * g_ref[...] + b_ref[...]
    of_ref[...] = out
    ob_ref[...] = out.astype(BF16)


def _deepnorm_ln(x, f, g, b, alpha, *, tm=256):
    m, d = x.shape
    tm = _tile(m, tm)
    row = pl.BlockSpec((tm, d), lambda i: (i, 0))
    vec = pl.BlockSpec((1, d), lambda i: (0, 0))
    return pl.pallas_call(
        functools.partial(_ln_body, alpha=alpha),
        grid=(m // tm,), in_specs=[row, row, vec, vec], out_specs=(row, row),
        out_shape=(jax.ShapeDtypeStruct((m, d), F32), jax.ShapeDtypeStruct((m, d), BF16)),
        compiler_params=_cparams(1))(x, f, g.reshape(1, d), b.reshape(1, d))


CONV_CH_CHUNK = 256
CONV_ROW_CHUNK = 32
A_PAD = 8
B_PAD = 32
LN_ROW_CHUNK = 16


def _convmix_body(gb_ref, gc_ref, h_ref, ga_ref, gg_ref, ctxa_ref, ctxb_ref, wa_ref, wb_ref,
                  bb_ref, ng_ref, nb_ref, mix_ref, sa_ref, sb_ref, exta, extb, cbuf, *, tt, d):
    t = pl.program_id(1)
    a0 = A_PAD - (CONV_A_W - 1)
    b0 = B_PAD - (CONV_B_W - 1)

    @pl.when(t == 0)
    def _():
        exta[a0:A_PAD, :] = ctxa_ref[0]
        extb[b0:B_PAD, :] = ctxb_ref[0]

    exta[A_PAD:A_PAD + tt, :] = gc_ref[0] * h_ref[0]
    extb[B_PAD:B_PAD + tt, :] = ga_ref[0] * jax.nn.sigmoid(gg_ref[0])

    cw = min(CONV_CH_CHUNK, d)
    rc = min(CONV_ROW_CHUNK, tt)

    def chunk(cc, carry):
        c0 = pl.multiple_of(cc * cw, cw)
        cols = pl.ds(c0, cw)
        for r0 in range(0, tt, rc):
            acc = wa_ref[0:1, cols] * exta[a0 + r0:a0 + r0 + rc, cols]
            for k in range(1, CONV_A_W):
                acc = acc + wa_ref[k:k + 1, cols] * exta[a0 + r0 + k:a0 + r0 + k + rc, cols]
            mix_ref[0, r0:r0 + rc, cols] = (gb_ref[0, r0:r0 + rc, cols] * acc).astype(BF16)
            accb = wb_ref[0:1, cols] * extb[b0 + r0:b0 + r0 + rc, cols]
            for k in range(1, CONV_B_W):
                accb = accb + wb_ref[k:k + 1, cols] * extb[b0 + r0 + k:b0 + r0 + k + rc, cols]
            cbuf[r0:r0 + rc, cols] = accb + bb_ref[0:1, cols]
        return carry

    lax.fori_loop(0, d // cw, chunk, 0)

    lr = min(LN_ROW_CHUNK, tt)

    def ln_rows(i, carry):
        r0 = pl.multiple_of(i * lr, lr)
        c = cbuf[pl.ds(r0, lr), :]
        mu = jnp.mean(c, axis=-1, keepdims=True)
        cc = c - mu
        var = jnp.mean(cc * cc, axis=-1, keepdims=True)
        y = cc * lax.rsqrt(var + LN_EPS) * ng_ref[...] + nb_ref[...]
        mix_ref[0, pl.ds(r0, lr), d:2 * d] = (y * jax.nn.sigmoid(y)).astype(BF16)
        return carry

    lax.fori_loop(0, tt // lr, ln_rows, 0)

    ta = exta[a0 + tt:A_PAD + tt, :]
    tb = extb[b0 + tt:B_PAD + tt, :]
    exta[a0:A_PAD, :] = ta
    extb[b0:B_PAD, :] = tb

    @pl.when(t == pl.num_programs(1) - 1)
    def _():
        sa_ref[0] = ta
        sb_ref[0] = tb


def _conv_mixer_prompt(proj, ctx_a, ctx_b, wa, wb, bb, ng, nb, *, tt=256):
    bsz, t, _ = proj.shape
    d = wa.shape[1]
    tt = _tile(t, tt)
    assert tt >= CONV_B_W - 1 and tt % min(CONV_ROW_CHUNK, tt) == 0 and d % min(CONV_CH_CHUNK, d) == 0
    col = lambda c: pl.BlockSpec((1, tt, d), lambda b, i, c=c: (b, i, c))
    full = lambda r: pl.BlockSpec((r, d), lambda b, i: (0, 0))
    st = lambda r: pl.BlockSpec((1, r, d), lambda b, i: (b, 0, 0))
    return pl.pallas_call(
        functools.partial(_convmix_body, tt=tt, d=d),
        grid=(bsz, t // tt),
        in_specs=[col(0), col(1), col(2), col(3), col(4), st(CONV_A_W - 1), st(CONV_B_W - 1),
                  full(CONV_A_W), full(CONV_B_W), full(1), full(1), full(1)],
        out_specs=(pl.BlockSpec((1, tt, 2 * d), lambda b, i: (b, i, 0)), st(CONV_A_W - 1), st(CONV_B_W - 1)),
        out_shape=(jax.ShapeDtypeStruct((bsz, t, 2 * d), BF16),
                   jax.ShapeDtypeStruct((bsz, CONV_A_W - 1, d), F32),
                   jax.ShapeDtypeStruct((bsz, CONV_B_W - 1, d), F32)),
        scratch_shapes=[pltpu.VMEM((A_PAD + tt, d), F32), pltpu.VMEM((B_PAD + tt, d), F32),
                        pltpu.VMEM((tt, d), F32)],
        compiler_params=_cparams(2),
    )(proj, proj, proj, proj, proj, ctx_a, ctx_b, wa, wb, bb.reshape(1, d), ng.reshape(1, d), nb.reshape(1, d))


def _convmix_dec_body(proj_ref, ctxa_ref, ctxb_ref, wa_ref, wb_ref, bb_ref, ng_ref, nb_ref,
                      mix_ref, sa_ref, sb_ref, *, d):
    gate_b = proj_ref[:, 0:d]
    u = proj_ref[:, d:2 * d] * proj_ref[:, 2 * d:3 * d]
    glu = proj_ref[:, 3 * d:4 * d] * jax.nn.sigmoid(proj_ref[:, 4 * d:5 * d])
    acc = wa_ref[CONV_A_W - 1:CONV_A_W, :] * u
    for k in range(CONV_A_W - 1):
        acc = acc + wa_ref[k:k + 1, :] * ctxa_ref[k]
    mix_ref[:, 0:d] = (gate_b * acc).astype(BF16)
    accb = wb_ref[CONV_B_W - 1:CONV_B_W, :] * glu + bb_ref[...]
    for k in range(CONV_B_W - 1):
        accb = accb + wb_ref[k:k + 1, :] * ctxb_ref[k]
    mu = jnp.mean(accb, axis=-1, keepdims=True)
    cc = accb - mu
    var = jnp.mean(cc * cc, axis=-1, keepdims=True)
    y = cc * lax.rsqrt(var + LN_EPS) * ng_ref[...] + nb_ref[...]
    mix_ref[:, d:2 * d] = (y * jax.nn.sigmoid(y)).astype(BF16)
    for k in range(CONV_A_W - 2):
        sa_ref[k] = ctxa_ref[k + 1]
    sa_ref[CONV_A_W - 2] = u
    for k in range(CONV_B_W - 2):
        sb_ref[k] = ctxb_ref[k + 1]
    sb_ref[CONV_B_W - 2] = glu


def _conv_mixer_decode(proj, ctx_a_t, ctx_b_t, wa, wb, bb, ng, nb):
    s = proj.shape[0]
    d = wa.shape[1]
    return pl.pallas_call(
        functools.partial(_convmix_dec_body, d=d),
        out_shape=(jax.ShapeDtypeStruct((s, 2 * d), BF16),
                   jax.ShapeDtypeStruct((CONV_A_W - 1, s, d), F32),
                   jax.ShapeDtypeStruct((CONV_B_W - 1, s, d), F32)),
        compiler_params=pltpu.CompilerParams(vmem_limit_bytes=VMEM_LIMIT_BYTES),
    )(proj, ctx_a_t, ctx_b_t, wa, wb, bb.reshape(1, d), ng.reshape(1, d), nb.reshape(1, d))


FFN_PAD = 8


def _gelu(x):
    return 0.5 * x * (1.0 + lax.erf(x * (1.0 / math.sqrt(2.0))))


FFN_ROW_BLOCK = 512


def _ffn_up_body(x_ref, wg_ref, wv_ref, cw_ref, cb_ref, ctx_ref, h_ref, st_ref, wgk_ref, wvk_ref, wb, ext, vbuf,
                 *, tm, rb):
    t = pl.program_id(2)
    e0 = FFN_PAD - (FFN_CONV_W - 1)
    tn = wg_ref.shape[-1]

    @pl.when(jnp.logical_and(pl.program_id(1) == 0, t == 0))
    def _():
        wb[:, 0:tn] = wg_ref[...].astype(BF16)
        wb[:, tn:2 * tn] = wv_ref[...].astype(BF16)
        wgk_ref[...] = wb[:, 0:tn]
        wvk_ref[...] = wb[:, tn:2 * tn]

    @pl.when(t == 0)
    def _():
        ext[e0:FFN_PAD, :] = ctx_ref[0]

    def matmuls(r):
        gv = jnp.dot(x_ref[0, r * rb:(r + 1) * rb, :], wb[...], preferred_element_type=F32)
        ext[FFN_PAD + r * rb:FFN_PAD + (r + 1) * rb, :] = gv[:, 0:tn]
        vbuf[r * rb:(r + 1) * rb, :] = gv[:, tn:2 * tn]

    rc = min(CONV_ROW_CHUNK, rb)

    def epilogue(r):
        for r0 in range(r * rb, (r + 1) * rb, rc):
            acc = cw_ref[0:1, :] * ext[e0 + r0:e0 + r0 + rc, :] + cb_ref[...]
            for k in range(1, FFN_CONV_W):
                acc = acc + cw_ref[k:k + 1, :] * ext[e0 + r0 + k:e0 + r0 + k + rc, :]
            h_ref[0, r0:r0 + rc, :] = (_gelu(acc) * vbuf[r0:r0 + rc, :]).astype(BF16)

    nsub = tm // rb
    matmuls(0)
    for r in range(1, nsub + 1):
        epilogue(r - 1)
        if r < nsub:
            matmuls(r)

    tail = ext[e0 + tm:FFN_PAD + tm, :]
    ext[e0:FFN_PAD, :] = tail

    @pl.when(t == pl.num_programs(2) - 1)
    def _():
        st_ref[0] = tail


def _ffn_up_prompt(x, w_up, layer, cw, cb, ctx, *, tm=1024, tn=256):
    bsz, t, k = x.shape
    f = cw.shape[1]
    tm = _tile(t, tm)
    rb = _tile(tm, FFN_ROW_BLOCK)
    tn = _tile(f, tn, LANES)
    nc = f // tn
    assert rb % min(CONV_ROW_CHUNK, rb) == 0
    return pl.pallas_call(
        functools.partial(_ffn_up_body, tm=tm, rb=rb),
        grid=(nc, bsz, t // tm),
        in_specs=[pl.BlockSpec((1, tm, k), lambda c, b, i: (b, i, 0)),
                  pl.BlockSpec((None, k, tn), lambda c, b, i: (layer, 0, c)),
                  pl.BlockSpec((None, k, tn), lambda c, b, i: (layer, 0, nc + c)),
                  pl.BlockSpec((FFN_CONV_W, tn), lambda c, b, i: (0, c)),
                  pl.BlockSpec((1, tn), lambda c, b, i: (0, c)),
                  pl.BlockSpec((1, FFN_CONV_W - 1, tn), lambda c, b, i: (b, 0, c))],
        out_specs=(pl.BlockSpec((1, tm, tn), lambda c, b, i: (b, i, c)),
                   pl.BlockSpec((1, FFN_CONV_W - 1, tn), lambda c, b, i: (b, 0, c)),
                   pl.BlockSpec((k, tn), lambda c, b, i: (0, c)),
                   pl.BlockSpec((k, tn), lambda c, b, i: (0, c))),
        out_shape=(jax.ShapeDtypeStruct((bsz, t, f), BF16),
                   jax.ShapeDtypeStruct((bsz, FFN_CONV_W - 1, f), F32),
                   jax.ShapeDtypeStruct((k, f), BF16), jax.ShapeDtypeStruct((k, f), BF16)),
        scratch_shapes=[pltpu.VMEM((k, 2 * tn), BF16),
                        pltpu.VMEM((FFN_PAD + tm, tn), F32), pltpu.VMEM((tm, tn), F32)],
        compiler_params=_cparams(3),
    )(x, w_up, w_up, cw, cb.reshape(1, f), ctx)


def _ffn_up_dec_body(x_ref, wg_ref, wv_ref, cw_ref, cb_ref, ctx_ref, h_ref, st_ref):
    x = x_ref[...]
    g = jnp.dot(x, wg_ref[...], preferred_element_type=F32)
    v = jnp.dot(x, wv_ref[...], preferred_element_type=F32)
    acc = cw_ref[FFN_CONV_W - 1:FFN_CONV_W, :] * g + cb_ref[...]
    for k in range(FFN_CONV_W - 1):
        acc = acc + cw_ref[k:k + 1, :] * ctx_ref[k]
    h_ref[...] = (_gelu(acc) * v).astype(BF16)
    for k in range(FFN_CONV_W - 2):
        st_ref[k] = ctx_ref[k + 1]
    st_ref[FFN_CONV_W - 2] = g


def _ffn_up_decode(x, w_gate, w_value, cw, cb, ctx_t, *, tn=512):
    s, k = x.shape
    f = cw.shape[1]
    tn = _tile(f, tn, LANES)
    nc = f // tn
    return pl.pallas_call(
        _ffn_up_dec_body,
        grid=(nc,),
        in_specs=[pl.BlockSpec((s, k), lambda c: (0, 0)),
                  pl.BlockSpec((k, tn), lambda c: (0, c)),
                  pl.BlockSpec((k, tn), lambda c: (0, c)),
                  pl.BlockSpec((FFN_CONV_W, tn), lambda c: (0, c)),
                  pl.BlockSpec((1, tn), lambda c: (0, c)),
                  pl.BlockSpec((FFN_CONV_W - 1, s, tn), lambda c: (0, 0, c))],
        out_specs=(pl.BlockSpec((s, tn), lambda c: (0, c)),
                   pl.BlockSpec((FFN_CONV_W - 1, s, tn), lambda c: (0, 0, c))),
        out_shape=(jax.ShapeDtypeStruct((s, f), BF16),
                   jax.ShapeDtypeStruct((FFN_CONV_W - 1, s, f), F32)),
        compiler_params=_cparams(1),
    )(x, w_gate, w_value, cw, cb.reshape(1, f), ctx_t)


def _qk(q, k):
    return lax.dot_general(q, k, (((1,), (1,)), ((), ())), preferred_element_type=F32)


def _softplus(z):
    return jnp.maximum(z, 0.0) + jnp.log(1.0 + jnp.exp(-jnp.abs(z)))


def _suffix_sums(lk, tri):
    hi = lk.astype(BF16)
    lo = (lk - hi.astype(F32)).astype(BF16)
    return (jnp.dot(hi, tri, preferred_element_type=F32) + jnp.dot(lo, tri, preferred_element_type=F32))


def _strict_lower_ones(n):
    r = lax.broadcasted_iota(jnp.int32, (n, n), 0)
    c = lax.broadcasted_iota(jnp.int32, (n, n), 1)
    return jnp.where(r > c, 1.0, 0.0).astype(BF16)


def _lambda(lq1_ref, lk1_ref, lq2_ref, lk2_ref, lambda_init):
    s1 = jnp.sum(lq1_ref[...] * lk1_ref[...], axis=-1, keepdims=True)
    s2 = jnp.sum(lq2_ref[...] * lk2_ref[...], axis=-1, keepdims=True)
    return jnp.exp(s1) - jnp.exp(s2) + lambda_init


def _sb_prompt_body(q_ref, k_ref, v_ref, o_ref, acc_ref, c_ref, *, tq, hp, scale):
    qi = pl.program_id(2)
    tri = _strict_lower_ones(tq)
    row = lax.broadcasted_iota(jnp.int32, (tq, tq), 0)
    col = lax.broadcasted_iota(jnp.int32, (tq, tq), 1)

    def block(kb, diagonal):
        ks = pl.ds(pl.multiple_of(kb * tq, tq), tq)
        heads = range(hp)
        cols = [slice(hh * HD, (hh + 1) * HD) for hh in heads]
        z = [_qk(q_ref[0, :, cols[hh]], k_ref[0, ks, cols[hh]]) * scale for hh in heads]
        sp = [_softplus(z[hh]) for hh in heads]
        lk = [jnp.where(col < row, -sp[hh], 0.0) if diagonal else -sp[hh] for hh in heads]
        within = [_suffix_sums(lk[hh], tri) for hh in heads]
        block_sum = [within[hh][:, 0:1] + lk[hh][:, 0:1] for hh in heads]
        if diagonal:
            a = [jnp.where(col < row, jnp.exp(z[hh] - sp[hh] + within[hh]), 0.0) for hh in heads]
            c_new = block_sum
        else:
            c_old = [c_ref[hh] for hh in heads]
            a = [jnp.exp(z[hh] - sp[hh] + (c_old[hh] + within[hh])) for hh in heads]
            c_new = [c_old[hh] + block_sum[hh] for hh in heads]
        pv = [jnp.dot(a[hh].astype(BF16), v_ref[0, ks, cols[hh]], preferred_element_type=F32) for hh in heads]
        if not diagonal:
            pv = [acc_ref[hh] + pv[hh] for hh in heads]
        for hh in heads:
            c_ref[hh] = c_new[hh]
            acc_ref[hh] = pv[hh]

    block(qi, True)

    def body(i, carry):
        block(qi - 1 - i, False)
        return carry

    lax.fori_loop(0, qi, body, 0)
    for hh in range(hp):
        o_ref[0, :, hh * HD:(hh + 1) * HD] = acc_ref[hh].astype(o_ref.dtype)


def _sb_attention_prompt(q, k, v, *, tq=256, hp=4):
    bsz, t, d = q.shape
    n_heads = d // HD
    tq = _tile(t, tq)
    hp = _tile(n_heads, hp, 1)
    w = hp * HD
    qspec = pl.BlockSpec((1, tq, w), lambda b, h, i: (b, i, h))
    kvspec = pl.BlockSpec((1, t, w), lambda b, h, i: (b, 0, h))
    return pl.pallas_call(
        functools.partial(_sb_prompt_body, tq=tq, hp=hp, scale=HD ** -0.5),
        grid=(bsz, n_heads // hp, t // tq),
        in_specs=[qspec, kvspec, kvspec], out_specs=qspec,
        out_shape=jax.ShapeDtypeStruct((bsz, t, d), BF16),
        scratch_shapes=[pltpu.VMEM((hp, tq, HD), F32), pltpu.VMEM((hp, tq, 1), F32)],
        compiler_params=_cparams(3),
    )(q, k, v)


def _df_prompt_body(q_ref, k_ref, v_ref, lq1_ref, lk1_ref, lq2_ref, lk2_ref, g_ref, o_ref,
                    m_ref, l_ref, acc_ref, *, tq, hp, scale, lambda_init):
    qi = pl.program_id(2)
    row = lax.broadcasted_iota(jnp.int32, (tq, tq), 0)
    col = lax.broadcasted_iota(jnp.int32, (tq, tq), 1)
    w = 2 * HD

    def block(kb, diagonal):
        ks = pl.ds(pl.multiple_of(kb * tq, tq), tq)
        maps = range(2 * hp)
        cols = [slice(n * HD, (n + 1) * HD) for n in maps]
        s = [_qk(q_ref[0, :, cols[n]], k_ref[0, ks, cols[n]]) * scale for n in maps]
        if diagonal:
            s = [jnp.where(col <= row, s[n], -jnp.inf) for n in maps]
            m_new = [jnp.max(s[n], axis=-1, keepdims=True) for n in maps]
        else:
            m_old = [m_ref[n] for n in maps]
            m_new = [jnp.maximum(m_old[n], jnp.max(s[n], axis=-1, keepdims=True)) for n in maps]
            alpha = [jnp.exp(m_old[n] - m_new[n]) for n in maps]
        p = [jnp.exp(s[n] - m_new[n]) for n in maps]
        l_new = [jnp.sum(p[n], axis=-1, keepdims=True) for n in maps]
        pv = [jnp.dot(p[n].astype(BF16), v_ref[0, ks, (n // 2) * w:(n // 2 + 1) * w], preferred_element_type=F32)
              for n in maps]
        if not diagonal:
            l_new = [alpha[n] * l_ref[n] + l_new[n] for n in maps]
            pv = [alpha[n] * acc_ref[n] + pv[n] for n in maps]
        for n in maps:
            m_ref[n] = m_new[n]
            l_ref[n] = l_new[n]
            acc_ref[n] = pv[n]

    block(qi, True)

    def body(i, carry):
        block(qi - 1 - i, False)
        return carry

    lax.fori_loop(0, qi, body, 0)

    lam = _lambda(lq1_ref, lk1_ref, lq2_ref, lk2_ref, lambda_init)
    for hh in range(hp):
        o = acc_ref[2 * hh] / l_ref[2 * hh] - lam * (acc_ref[2 * hh + 1] / l_ref[2 * hh + 1])
        y = o * lax.rsqrt(jnp.mean(o * o, axis=-1, keepdims=True) + RMS_EPS) * g_ref[...]
        o_ref[0, :, hh * w:(hh + 1) * w] = (y * (1.0 - lambda_init)).astype(o_ref.dtype)


def _df_attention_prompt(q, k, v, lam_params, subln_g, lambda_init, *, tq=256, hp=4):
    bsz, t, d = q.shape
    w = 2 * HD
    n_heads = d // w
    tq = _tile(t, tq)
    hp = _tile(n_heads, hp, 1)
    wb = hp * w
    qspec = pl.BlockSpec((1, tq, wb), lambda b, h, i: (b, i, h))
    kvspec = pl.BlockSpec((1, t, wb), lambda b, h, i: (b, 0, h))
    vec = pl.BlockSpec((1, HD), lambda b, h, i: (0, 0))
    return pl.pallas_call(
        functools.partial(_df_prompt_body, tq=tq, hp=hp, scale=HD ** -0.5, lambda_init=lambda_init),
        grid=(bsz, n_heads // hp, t // tq),
        in_specs=[qspec, kvspec, kvspec, vec, vec, vec, vec, pl.BlockSpec((1, w), lambda b, h, i: (0, 0))],
        out_specs=qspec,
        out_shape=jax.ShapeDtypeStruct((bsz, t, d), BF16),
        scratch_shapes=[pltpu.VMEM((2 * hp, tq, 1), F32), pltpu.VMEM((2 * hp, tq, 1), F32),
                        pltpu.VMEM((2 * hp, tq, w), F32)],
        compiler_params=_cparams(3),
    )(q, k, v, *[p.reshape(1, HD) for p in lam_params], subln_g.reshape(1, w))


DECODE_PAGES_PER_STEP = 4


def _own_head_rows(s, n_heads, tiles, diag):
    return [jnp.sum(jnp.where(diag, s[:, j * LANES:(j + 1) * LANES], 0.0), axis=0, keepdims=True)
            for j in range(tiles)]


def _spread_rows(a, r0, n_heads, tiles, diag):
    return jnp.concatenate(
        [jnp.where(diag, jnp.broadcast_to(a[r0 + j:r0 + j + 1, :], (n_heads, LANES)), 0.0) for j in range(tiles)],
        axis=1)


def _head_diag(n_heads):
    sub = lax.broadcasted_iota(jnp.int32, (n_heads, LANES), 0)
    lane = lax.broadcasted_iota(jnp.int32, (n_heads, LANES), 1)
    return lane % n_heads == sub


def _split_dot_rhs(lhs, rhs):
    hi = rhs.astype(BF16)
    lo = (rhs - hi.astype(F32)).astype(BF16)
    return jnp.dot(lhs, hi, preferred_element_type=F32) + jnp.dot(lhs, lo, preferred_element_type=F32)


def _sb_decode_body(pt_ref, q_ref, *refs, n_heads, page, group, scale):
    k_refs = refs[:group]
    v_refs = refs[group:2 * group]
    o_ref, acc_ref, c_ref = refs[2 * group:]
    p = pl.program_id(1)
    tiles = page * n_heads // LANES
    nrow = group * tiles

    @pl.when(p == 0)
    def _():
        acc_ref[...] = jnp.zeros_like(acc_ref)
        c_ref[...] = jnp.zeros_like(c_ref)

    q = q_ref[0]
    diag = _head_diag(n_heads)
    zrows = []
    for g in range(group):
        zrows += _own_head_rows(_qk(q, k_refs[g][...].astype(BF16)), n_heads, tiles, diag)
    z = jnp.concatenate(zrows, axis=0) * scale
    sp = _softplus(z)
    lk = -sp

    li = lax.broadcasted_iota(jnp.int32, (LANES, LANES), 0)
    lj = lax.broadcasted_iota(jnp.int32, (LANES, LANES), 1)
    same_head = li % n_heads == lj % n_heads
    later_in_row = jnp.where(jnp.logical_and(same_head, li // n_heads > lj // n_heads), 1.0, 0.0).astype(BF16)
    whole_row = jnp.where(same_head, 1.0, 0.0).astype(BF16)
    within = _suffix_sums(lk, later_in_row)
    row_tot = _suffix_sums(lk, whole_row)
    ri = lax.broadcasted_iota(jnp.int32, (nrow, nrow), 0)
    rj = lax.broadcasted_iota(jnp.int32, (nrow, nrow), 1)
    later_rows = _split_dot_rhs(jnp.where(rj > ri, 1.0, 0.0).astype(BF16), row_tot)
    a = jnp.exp(z - sp + (c_ref[...] + later_rows + within))
    c_ref[...] = c_ref[...] + later_rows[0:1, :] + row_tot[0:1, :]

    upd = jnp.zeros((n_heads, HD), F32)
    for g in range(group):
        a_sp = _spread_rows(a, g * tiles, n_heads, tiles, diag).astype(BF16)
        upd = upd + jnp.dot(a_sp, v_refs[g][...].astype(BF16), preferred_element_type=F32)
    acc_ref[...] = acc_ref[...] + upd

    @pl.when(p == pl.num_programs(1) - 1)
    def _():
        o_ref[0] = acc_ref[...].astype(o_ref.dtype)


def _page_specs(block, page_table_cols, group, layer):
    def spec(g):
        return pl.BlockSpec(block, lambda b, p, pt: (pt[b, page_table_cols - (p + 1) * group + g], layer, 0, 0))
    return [spec(g) for g in range(group)]


def _sb_attention_decode(q, cache_k, cache_v, page_table, layer):
    s, n_heads, _ = q.shape
    page = cache_k.shape[2] // n_heads
    n_pages = page_table.shape[1]
    group = _tile(n_pages, DECODE_PAGES_PER_STEP, 1)
    assert (page * n_heads) % LANES == 0 and LANES % n_heads == 0
    kv = _page_specs((None, None, page * n_heads, HD), n_pages, group, layer)
    hd = pl.BlockSpec((1, n_heads, HD), lambda b, p, pt: (b, 0, 0))
    return pl.pallas_call(
        functools.partial(_sb_decode_body, n_heads=n_heads, page=page, group=group, scale=HD ** -0.5),
        grid_spec=pltpu.PrefetchScalarGridSpec(
            num_scalar_prefetch=1, grid=(s, n_pages // group), in_specs=[hd] + kv + kv, out_specs=hd,
            scratch_shapes=[pltpu.VMEM((n_heads, HD), F32), pltpu.VMEM((1, LANES), F32)]),
        out_shape=jax.ShapeDtypeStruct((s, n_heads, HD), BF16),
        compiler_params=_cparams(2),
    )(page_table, q, *([cache_k] * group), *([cache_v] * group))


def _head_lanes_max(x, n_heads):
    sh = n_heads
    while sh < LANES:
        x = jnp.maximum(x, pltpu.roll(x, sh, 1))
        sh *= 2
    return x


def _head_lanes_sum(x, n_heads):
    sh = n_heads
    while sh < LANES:
        x = x + pltpu.roll(x, sh, 1)
        sh *= 2
    return x


def _lanes_to_rows(x, n_heads, diag):
    lane = lax.broadcasted_iota(jnp.int32, (n_heads, LANES), 1)
    pick = jnp.logical_and(diag, lane < n_heads)
    return jnp.sum(jnp.where(pick, jnp.broadcast_to(x, (n_heads, LANES)), 0.0), axis=1, keepdims=True)


def _df_decode_body(pt_ref, q_ref, kn_ref, vn_ref, *refs, n_heads, page, group, scale, lambda_init):
    k_refs = refs[:group]
    v_refs = refs[group:2 * group]
    lq1_ref, lk1_ref, lq2_ref, lk2_ref, g_ref, o_ref, m_ref, l_ref, acc_ref = refs[2 * group:]
    p = pl.program_id(1)
    tiles = page * n_heads // LANES
    rows_per_map = page * n_heads

    @pl.when(p == 0)
    def _():
        m_ref[...] = jnp.full_like(m_ref, -jnp.inf)
        l_ref[...] = jnp.zeros_like(l_ref)
        acc_ref[...] = jnp.zeros_like(acc_ref)

    diag = _head_diag(n_heads)
    values = [v_refs[g][...].astype(BF16) for g in range(group)]
    srows = [[], []]
    for c in range(2):
        qc = q_ref[0, c * n_heads:(c + 1) * n_heads, :]
        for g in range(group):
            kc = k_refs[g][pl.ds(c, rows_per_map, stride=2), :].astype(BF16)
            srows[c] += _own_head_rows(_qk(qc, kc), n_heads, tiles, diag)
    s = [jnp.concatenate(srows[c], axis=0) * scale for c in range(2)]
    m_old = [m_ref[c] for c in range(2)]
    m_new = [jnp.maximum(m_old[c], _head_lanes_max(jnp.max(s[c], axis=0, keepdims=True), n_heads))
             for c in range(2)]
    alpha = [jnp.exp(m_old[c] - m_new[c]) for c in range(2)]
    pr = [jnp.exp(s[c] - m_new[c]) for c in range(2)]
    l_new = [alpha[c] * l_ref[c] + jnp.sum(pr[c], axis=0, keepdims=True) for c in range(2)]
    upd = [jnp.zeros((n_heads, 2 * HD), F32) for c in range(2)]
    for g in range(group):
        for c in range(2):
            p_sp = _spread_rows(pr[c], g * tiles, n_heads, tiles, diag).astype(BF16)
            upd[c] = upd[c] + jnp.dot(p_sp, values[g], preferred_element_type=F32)
    acc_new = [_lanes_to_rows(alpha[c], n_heads, diag) * acc_ref[c] + upd[c] for c in range(2)]
    for c in range(2):
        m_ref[c] = m_new[c]
        l_ref[c] = l_new[c]
        acc_ref[c] = acc_new[c]

    @pl.when(p == pl.num_programs(1) - 1)
    def _():
        outs = []
        for c in range(2):
            rows = slice(c * n_heads, (c + 1) * n_heads)
            m_col = _lanes_to_rows(m_ref[c], n_heads, diag)
            l_col = _lanes_to_rows(_head_lanes_sum(l_ref[c], n_heads), n_heads, diag)
            s_new = jnp.sum(q_ref[0, rows, :].astype(F32) * kn_ref[0, rows, :], axis=-1, keepdims=True) * scale
            m_fin = jnp.maximum(m_col, s_new)
            a_fin = jnp.exp(m_col - m_fin)
            p_new = jnp.exp(s_new - m_fin)
            outs.append((a_fin * acc_ref[c] + p_new * vn_ref[0]) / (a_fin * l_col + p_new))
        lam = _lambda(lq1_ref, lk1_ref, lq2_ref, lk2_ref, lambda_init)
        o = outs[0] - lam * outs[1]
        y = o * lax.rsqrt(jnp.mean(o * o, axis=-1, keepdims=True) + RMS_EPS) * g_ref[...]
        o_ref[0] = (y * (1.0 - lambda_init)).astype(o_ref.dtype)


def _df_attention_decode(q, k_new, v_new, cache_k, cache_v, page_table, layer, lam_params, subln_g, lambda_init):
    s, nr, _ = q.shape
    n_heads = nr // 2
    page = cache_v.shape[2] // n_heads
    n_pages = page_table.shape[1]
    group = _tile(n_pages, DECODE_PAGES_PER_STEP, 1)
    assert (page * n_heads) % LANES == 0 and LANES % n_heads == 0
    w = 2 * HD
    kspecs = _page_specs((None, None, page * nr, HD), n_pages, group, layer)
    vspecs = _page_specs((None, None, page * n_heads, w), n_pages, group, layer)
    qspec = pl.BlockSpec((1, nr, HD), lambda b, p, pt: (b, 0, 0))
    vnspec = pl.BlockSpec((1, n_heads, w), lambda b, p, pt: (b, 0, 0))
    vec = pl.BlockSpec((1, HD), lambda b, p, pt: (0, 0))
    return pl.pallas_call(
        functools.partial(_df_decode_body, n_heads=n_heads, page=page, group=group, scale=HD ** -0.5,
                          lambda_init=lambda_init),
        grid_spec=pltpu.PrefetchScalarGridSpec(
            num_scalar_prefetch=1, grid=(s, n_pages // group),
            in_specs=[qspec, qspec, vnspec] + kspecs + vspecs + [vec, vec, vec, vec,
                                                                 pl.BlockSpec((1, w), lambda b, p, pt: (0, 0))],
            out_specs=vnspec,
            scratch_shapes=[pltpu.VMEM((2, 1, LANES), F32), pltpu.VMEM((2, 1, LANES), F32),
                            pltpu.VMEM((2, n_heads, w), F32)]),
        out_shape=jax.ShapeDtypeStruct((s, n_heads, w), BF16),
        compiler_params=_cparams(2),
    )(page_table, q, k_new, v_new, *([cache_k] * group), *([cache_v] * group),
      *[p.reshape(1, HD) for p in lam_params], subln_g.reshape(1, w))


def _rope_tables(positions):
    half = HD // 2
    inv_freq = ROPE_THETA ** (-jnp.arange(half, dtype=F32) / half)
    ang = positions.astype(F32)[:, None] * inv_freq[None, :]
    cos, sin = jnp.cos(ang), jnp.sin(ang)
    return jnp.concatenate([cos, cos], axis=-1), jnp.concatenate([-sin, sin], axis=-1)


def kernel(x_prompt, x_sample, state_conv_a, state_conv_b, state_ffn, cache_sb_k, cache_sb_v, cache_df_k, cache_df_v, page_table, w_in_conv, conv_a_w, conv_b_w, conv_b_b, norm_b_g, norm_b_b, w_out_conv, w_in_attn, lambda_q1, lambda_k1, lambda_q2, lambda_k2, subln_g, w_out_attn, ln1_g, ln1_b, ln2_g, ln2_b, w_ffn_up, ffn_conv_w, ffn_conv_b, w_ffn_down):
    bp, t, dm = x_prompt.shape
    bs = x_sample.shape[0]
    assert x_sample.shape[1] == 1
    depth = ln1_g.shape[0]
    d_a = conv_a_w.shape[2]
    d_ff = ffn_conv_w.shape[2]
    h_sb = cache_sb_k.shape[3]
    h_df = cache_df_v.shape[3]
    d_sb = h_sb * HD
    d_df = h_df * 2 * HD
    page = cache_sb_k.shape[2]
    past_len = page_table.shape[1] * page
    alpha = (2 * depth) ** 0.25
    mp = bp * t

    hp = x_prompt.reshape(mp, dm)
    hs = x_sample.reshape(bs, dm)
    hp_b = hp.astype(BF16)
    hs_b = hs.astype(BF16)
    w_down_b = w_ffn_down.astype(BF16)

    ca_p, ca_s, cb_p, cb_s, f_p, f_s = [], [], [], [], [], []
    rows_p = ([], [], [], [])
    rows_s = ([], [], [], [])

    for layer in range(depth):
        i = layer // 2
        if layer % 2 == 0:
            w_in, w_out = w_in_conv[i], w_out_conv[i]
            cw = (conv_a_w[i], conv_b_w[i], conv_b_b[i], norm_b_g[i], norm_b_b[i])
            proj_p, w_in_b = _matmul([hp_b], w_in, (F32,), tm=PROJ_TM, tn=PROJ_TN, keep_w=True)
            mixed_p, sa, sb = _conv_mixer_prompt(
                proj_p.reshape(bp, t, 5 * d_a), jnp.zeros((bp, CONV_A_W - 1, d_a), F32),
                jnp.zeros((bp, CONV_B_W - 1, d_a), F32), *cw)
            ca_p.append(sa)
            cb_p.append(sb)
            m_p, w_out_b = _matmul([mixed_p.reshape(mp, 2 * d_a)], w_out, (F32,), tm=PROJ_TM, tn=PROJ_TN,
                                   keep_w=True)

            (proj_s,) = _matmul([hs_b], w_in_b, (F32,), tm=bs, tn=DEC_TN)
            mixed_s, sa, sb = _conv_mixer_decode(
                proj_s, jnp.swapaxes(state_conv_a[i], 0, 1), jnp.swapaxes(state_conv_b[i], 0, 1), *cw)
            ca_s.append(jnp.swapaxes(sa, 0, 1))
            cb_s.append(jnp.swapaxes(sb, 0, 1))
            (m_s,) = _matmul([mixed_s], w_out_b, (F32,), tm=bs, tn=DEC_TN)
        else:
            lambda_init = 0.8 - 0.6 * math.exp(-0.3 * layer)
            w_in, w_out = w_in_attn[i], w_out_attn[i]
            lam_params = (lambda_q1[i], lambda_k1[i], lambda_q2[i], lambda_k2[i])
            edges = (0, d_sb, 2 * d_sb, 3 * d_sb, 3 * d_sb + d_df, 3 * d_sb + 2 * d_df, 3 * d_sb + 3 * d_df)
            span = lambda n: (edges[n], edges[n + 1])

            def qkv_proj(x_b, tm, tn, rope, weights):
                outs = []
                for n in range(6):
                    dts = (BF16,) if n in (0, 3) else (F32, BF16)
                    r = rope if n in (3, 4) else None
                    if isinstance(weights, tuple):
                        outs.append(_matmul([x_b], weights[n], dts, tm=tm, tn=tn, rope=r))
                    else:
                        outs.append(_matmul([x_b], weights, dts, tm=tm, tn=tn, rope=r, cols=span(n), keep_w=True))
                return outs

            cos_p, sin_p = _rope_tables(jnp.arange(t, dtype=jnp.int32))
            groups = qkv_proj(hp_b, PROJ_TM, PROJ_TN, (cos_p, sin_p, t), w_in)
            w_in_b = tuple(g[-1] for g in groups)
            (q_sb,), k_sb, v_sb, (q_df,), k_df, v_df = [g[:-1] for g in groups]
            new_p = (k_sb[0].reshape(bp, t, h_sb, HD), v_sb[0].reshape(bp, t, h_sb, HD),
                     k_df[0].reshape(bp, t, h_df, 2, HD), v_df[0].reshape(bp, t, h_df, 2 * HD))
            b3 = lambda a: a.reshape(bp, t, -1)
            o_sb = _sb_attention_prompt(b3(q_sb), b3(k_sb[1]), b3(v_sb[1]))
            o_df = _df_attention_prompt(b3(q_df), b3(k_df[1]), b3(v_df[1]), lam_params, subln_g[i], lambda_init)
            m_p, w_out_b = _matmul([o_sb.reshape(mp, d_sb), o_df.reshape(mp, d_df)], w_out, (F32,),
                                   tm=PROJ_TM, tn=PROJ_TN, keep_w=True)

            cos_s, sin_s = _rope_tables(jnp.full((1,), past_len, jnp.int32))
            (q_sb,), k_sb, v_sb, (q_df,), k_df, v_df = qkv_proj(hs_b, bs, DEC_TN, (cos_s, sin_s, 1), w_in_b)
            new_s = (k_sb[0].reshape(bs, 1, h_sb, HD), v_sb[0].reshape(bs, 1, h_sb, HD),
                     k_df[0].reshape(bs, 1, h_df, 2, HD), v_df[0].reshape(bs, 1, h_df, 2 * HD))
            pool, n_layers = cache_sb_k.shape[:2]
            o_sb_s = _sb_attention_decode(
                q_sb.reshape(bs, h_sb, HD), cache_sb_k.reshape(pool, n_layers, page * h_sb, HD),
                cache_sb_v.reshape(pool, n_layers, page * h_sb, HD), page_table, i)
            q_dec = jnp.swapaxes(q_df.reshape(bs, h_df, 2, HD), 1, 2).reshape(bs, 2 * h_df, HD)
            k_dec = jnp.swapaxes(k_df[0].reshape(bs, h_df, 2, HD), 1, 2).reshape(bs, 2 * h_df, HD)
            o_df_s = _df_attention_decode(
                q_dec, k_dec, v_df[0].reshape(bs, h_df, 2 * HD),
                cache_df_k.reshape(pool, n_layers, page * h_df * 2, HD),
                cache_df_v.reshape(pool, n_layers, page * h_df, 2 * HD), page_table, i,
                lam_params, subln_g[i], lambda_init)
            (m_s,) = _matmul([o_sb_s.reshape(bs, d_sb), o_df_s.reshape(bs, d_df)], w_out_b, (F32,),
                             tm=bs, tn=DEC_TN)
            for lst, r in zip(rows_p, new_p):
                lst.append(r)
            for lst, r in zip(rows_s, new_s):
                lst.append(r)

        hp, hp_b = _deepnorm_ln(hp, m_p, ln1_g[layer], ln1_b[layer], alpha)
        hs, hs_b = _deepnorm_ln(hs, m_s, ln1_g[layer], ln1_b[layer], alpha)

        h_p, st_p, w_gate_b, w_value_b = _ffn_up_prompt(
            hp_b.reshape(bp, t, dm), w_ffn_up, layer, ffn_conv_w[layer], ffn_conv_b[layer],
            jnp.zeros((bp, FFN_CONV_W - 1, d_ff), F32))
        f_p.append(st_p)
        (fp,) = _matmul([h_p.reshape(mp, d_ff)], w_down_b, (F32,), tm=512, tn=512, layer=layer)
        h_s, st_s = _ffn_up_decode(hs_b, w_gate_b, w_value_b, ffn_conv_w[layer], ffn_conv_b[layer],
                                   jnp.swapaxes(state_ffn[layer], 0, 1))
        f_s.append(jnp.swapaxes(st_s, 0, 1))
        (fs,) = _matmul([h_s], w_down_b, (F32,), tm=bs, tn=512, layer=layer)

        hp, hp_b = _deepnorm_ln(hp, fp, ln2_g[layer], ln2_b[layer], alpha)
        hs, hs_b = _deepnorm_ln(hs, fs, ln2_g[layer], ln2_b[layer], alpha)

    return (hp.reshape(bp, t, dm), hs.reshape(bs, 1, dm),
            jnp.stack(ca_p), jnp.stack(ca_s), jnp.stack(cb_p), jnp.stack(cb_s),
            jnp.stack(f_p), jnp.stack(f_s),
            jnp.stack(rows_p[0], axis=1), jnp.stack(rows_s[0], axis=1),
            jnp.stack(rows_p[1], axis=1), jnp.stack(rows_s[1], axis=1),
            jnp.stack(rows_p[2], axis=1), jnp.stack(rows_s[2], axis=1),
            jnp.stack(rows_p[3], axis=1), jnp.stack(rows_s[3], axis=1))
```

```python
import functools
import math

import jax
import jax.numpy as jnp
from jax import lax
from jax.experimental import pallas as pl
from jax.experimental.pallas import tpu as pltpu

F32 = jnp.float32
BF16 = jnp.bfloat16

HD = 128
CONV_A_W = 3
CONV_B_W = 31
FFN_CONV_W = 3
ROPE_THETA = 10000.0
LN_EPS = 1e-5
RMS_EPS = 1e-5

LANES = 128
SUBLANES = 8
VMEM_LIMIT_BYTES = 56 * 1024 * 1024


def _cparams(n_grid):
    return pltpu.CompilerParams(dimension_semantics=("arbitrary",) * n_grid,
                                vmem_limit_bytes=VMEM_LIMIT_BYTES)


def _tile(dim, pref, align=SUBLANES):
    if dim <= pref:
        return dim
    t = pref - pref % align
    while t >= align:
        if dim % t == 0:
            return t
        t -= align
    return dim


PROJ_TM = 1024
PROJ_TN = 512
DEC_TN = 1024


def _mm_body(*refs, n_x, n_out, rope, cast_w, keep_w, tn):
    x_refs = refs[:n_x]
    w_ref = refs[n_x]
    pos = n_x + 1
    if rope:
        cos_ref, sin_ref = refs[pos:pos + 2]
        pos += 2
    outs = refs[pos:pos + n_out]
    if cast_w:
        wb_ref = refs[-1]
        wf_ref = w_ref

        @pl.when(pl.program_id(1) == 0)
        def _():
            wb_ref[...] = wf_ref[...].astype(BF16)
            if keep_w:
                refs[pos + n_out][...] = wb_ref[...]

        w_ref = wb_ref
    acc = None
    k0 = 0
    for x_ref in x_refs:
        kk = x_ref.shape[1]
        part = jnp.dot(x_ref[...], w_ref[k0:k0 + kk, :], preferred_element_type=F32)
        acc = part if acc is None else acc + part
        k0 += kk
    if not rope:
        for o in outs:
            o[...] = acc.astype(o.dtype)
        return
    cos = cos_ref[...]
    sin = sin_ref[...]
    for c in range(tn // HD):
        xc = acc[:, c * HD:(c + 1) * HD]
        rc = xc * cos + pltpu.roll(xc, HD // 2, 1) * sin
        for o in outs:
            o[:, c * HD:(c + 1) * HD] = rc.astype(o.dtype)


def _matmul(xs, w, out_dtypes, *, tm, tn, cols=None, rope=None, layer=None, keep_w=False):
    xs = tuple(xs)
    m = xs[0].shape[0]
    k = sum(x.shape[1] for x in xs)
    assert k == w.shape[-2] and (layer is None) == (w.ndim == 2)
    lo, hi = cols if cols is not None else (0, w.shape[-1])
    n = hi - lo
    tm = _tile(m, tm)
    tn = _tile(math.gcd(n, lo) if lo else n, tn, LANES)
    assert lo % tn == 0 and n % tn == 0
    j0 = lo // tn
    cast_w = w.dtype != BF16
    assert cast_w or not keep_w
    in_specs = [pl.BlockSpec((tm, x.shape[1]), lambda j, i: (i, 0)) for x in xs]
    if layer is None:
        in_specs.append(pl.BlockSpec((k, tn), lambda j, i: (0, j0 + j)))
    else:
        in_specs.append(pl.BlockSpec((None, k, tn), lambda j, i: (layer, 0, j0 + j)))
    args = list(xs) + [w]
    if rope is not None:
        cos, sin, period = rope
        assert tn % HD == 0
        if period <= tm:
            assert tm % period == 0
            cos = jnp.tile(cos, (tm // period, 1))
            sin = jnp.tile(sin, (tm // period, 1))
            tspec = pl.BlockSpec((tm, HD), lambda j, i: (0, 0))
        else:
            assert period % tm == 0
            nper = period // tm
            tspec = pl.BlockSpec((tm, HD), lambda j, i: (i % nper, 0))
        in_specs += [tspec, tspec]
        args += [cos, sin]
    out_shape = tuple(jax.ShapeDtypeStruct((m, n), dt) for dt in out_dtypes)
    out_specs = tuple(pl.BlockSpec((tm, tn), lambda j, i: (i, j)) for _ in out_dtypes)
    if keep_w:
        out_shape += (jax.ShapeDtypeStruct((k, n), BF16),)
        out_specs += (pl.BlockSpec((k, tn), lambda j, i: (0, j)),)
    return pl.pallas_call(
        functools.partial(_mm_body, n_x=len(xs), n_out=len(out_dtypes), rope=rope is not None,
                          cast_w=cast_w, keep_w=keep_w, tn=tn),
        grid=(n // tn, m // tm), in_specs=in_specs, out_specs=out_specs, out_shape=out_shape,
        scratch_shapes=[pltpu.VMEM((k, tn), BF16)] if cast_w else [],
        compiler_params=_cparams(2))(*args)


def _ln_body(x_ref, f_ref, g_ref, b_ref, of_ref, ob_ref, *, alpha):
    y = alpha * x_ref[...] + f_ref[...]
    mu = jnp.mean(y, axis=-1, keepdims=True)
    yc = y - mu
    var = jnp.mean(yc * yc, axis=-1, keepdims=True)
    out = yc * lax.rsqrt(var + LN_EPS) * g_ref[...] + b_ref[...]
    of_ref[...] = out
    ob_ref[...] = out.astype(BF16)


def _deepnorm_ln(x, f, g, b, alpha, *, tm=256):
    m, d = x.shape
    tm = _tile(m, tm)
    row = pl.BlockSpec((tm, d), lambda i: (i, 0))
    vec = pl.BlockSpec((1, d), lambda i: (0, 0))
    return pl.pallas_call(
        functools.partial(_ln_body, alpha=alpha),
        grid=(m // tm,), in_specs=[row, row, vec, vec], out_specs=(row, row),
        out_shape=(jax.ShapeDtypeStruct((m, d), F32), jax.ShapeDtypeStruct((m, d), BF16)),
        compiler_params=_cparams(1))(x, f, g.reshape(1, d), b.reshape(1, d))


CONV_CH_CHUNK = 256
CONV_ROW_CHUNK = 32
A_PAD = 8
B_PAD = 32
LN_ROW_CHUNK = 16


def _convmix_body(gb_ref, gc_ref, h_ref, ga_ref, gg_ref, ctxa_ref, ctxb_ref, wa_ref, wb_ref,
                  bb_ref, ng_ref, nb_ref, mix_ref, sa_ref, sb_ref, exta, extb, cbuf, shifted, *, tt, d):
    t = pl.program_id(1)
    a0 = A_PAD - (CONV_A_W - 1)
    b0 = B_PAD - (CONV_B_W - 1)

    @pl.when(t == 0)
    def _():
        exta[a0:A_PAD, :] = ctxa_ref[0]
        extb[b0:B_PAD, :] = ctxb_ref[0]

    exta[A_PAD:A_PAD + tt, :] = gc_ref[0] * h_ref[0]
    extb[B_PAD:B_PAD + tt, :] = ga_ref[0] * jax.nn.sigmoid(gg_ref[0])

    cw = min(CONV_CH_CHUNK, d)
    rc = min(CONV_ROW_CHUNK, tt)

    def chunk(cc, carry):
        c0 = pl.multiple_of(cc * cw, cw)
        cols = pl.ds(c0, cw)
        for s in range(SUBLANES):
            span = tt + SUBLANES * ((CONV_B_W - 1 - s) // SUBLANES)
            shifted[s, 0:span, :] = extb[b0 + s:b0 + s + span, cols]
        for r0 in range(0, tt, rc):
            acc = wa_ref[0:1, cols] * exta[a0 + r0:a0 + r0 + rc, cols]
            for k in range(1, CONV_A_W):
                acc = acc + wa_ref[k:k + 1, cols] * exta[a0 + r0 + k:a0 + r0 + k + rc, cols]
            mix_ref[0, r0:r0 + rc, cols] = (gb_ref[0, r0:r0 + rc, cols] * acc).astype(BF16)
            accb = wb_ref[0:1, cols] * shifted[0, r0:r0 + rc, :]
            for k in range(1, CONV_B_W):
                ra = r0 + SUBLANES * (k // SUBLANES)
                accb = accb + wb_ref[k:k + 1, cols] * shifted[k % SUBLANES, ra:ra + rc, :]
            cbuf[r0:r0 + rc, cols] = accb + bb_ref[0:1, cols]
        return carry

    lax.fori_loop(0, d // cw, chunk, 0)

    lr = min(LN_ROW_CHUNK, tt)

    def ln_rows(i, carry):
        r0 = pl.multiple_of(i * lr, lr)
        c = cbuf[pl.ds(r0, lr), :]
        mu = jnp.mean(c, axis=-1, keepdims=True)
        cc = c - mu
        var = jnp.mean(cc * cc, axis=-1, keepdims=True)
        y = cc * lax.rsqrt(var + LN_EPS) * ng_ref[...] + nb_ref[...]
        mix_ref[0, pl.ds(r0, lr), d:2 * d] = (y * jax.nn.sigmoid(y)).astype(BF16)
        return carry

    lax.fori_loop(0, tt // lr, ln_rows, 0)

    ta = exta[a0 + tt:A_PAD + tt, :]
    tb = extb[b0 + tt:B_PAD + tt, :]
    exta[a0:A_PAD, :] = ta
    extb[b0:B_PAD, :] = tb

    @pl.when(t == pl.num_programs(1) - 1)
    def _():
        sa_ref[0] = ta
        sb_ref[0] = tb


def _conv_mixer_prompt(proj, ctx_a, ctx_b, wa, wb, bb, ng, nb, *, tt=256):
    bsz, t, _ = proj.shape
    d = wa.shape[1]
    tt = _tile(t, tt)
    assert tt >= CONV_B_W - 1 and tt % min(CONV_ROW_CHUNK, tt) == 0 and d % min(CONV_CH_CHUNK, d) == 0
    col = lambda c: pl.BlockSpec((1, tt, d), lambda b, i, c=c: (b, i, c))
    full = lambda r: pl.BlockSpec((r, d), lambda b, i: (0, 0))
    st = lambda r: pl.BlockSpec((1, r, d), lambda b, i: (b, 0, 0))
    return pl.pallas_call(
        functools.partial(_convmix_body, tt=tt, d=d),
        grid=(bsz, t // tt),
        in_specs=[col(0), col(1), col(2), col(3), col(4), st(CONV_A_W - 1), st(CONV_B_W - 1),
                  full(CONV_A_W), full(CONV_B_W), full(1), full(1), full(1)],
        out_specs=(pl.BlockSpec((1, tt, 2 * d), lambda b, i: (b, i, 0)), st(CONV_A_W - 1), st(CONV_B_W - 1)),
        out_shape=(jax.ShapeDtypeStruct((bsz, t, 2 * d), BF16),
                   jax.ShapeDtypeStruct((bsz, CONV_A_W - 1, d), F32),
                   jax.ShapeDtypeStruct((bsz, CONV_B_W - 1, d), F32)),
        scratch_shapes=[pltpu.VMEM((A_PAD + tt, d), F32), pltpu.VMEM((B_PAD + tt, d), F32),
                        pltpu.VMEM((tt, d), F32),
                        pltpu.VMEM((SUBLANES, tt + SUBLANES * ((CONV_B_W - 1) // SUBLANES), min(CONV_CH_CHUNK, d)), F32)],
        compiler_params=_cparams(2),
    )(proj, proj, proj, proj, proj, ctx_a, ctx_b, wa, wb, bb.reshape(1, d), ng.reshape(1, d), nb.reshape(1, d))


def _convmix_dec_body(proj_ref, ctxa_ref, ctxb_ref, wa_ref, wb_ref, bb_ref, ng_ref, nb_ref,
                      mix_ref, sa_ref, sb_ref, *, d):
    gate_b = proj_ref[:, 0:d]
    u = proj_ref[:, d:2 * d] * proj_ref[:, 2 * d:3 * d]
    glu = proj_ref[:, 3 * d:4 * d] * jax.nn.sigmoid(proj_ref[:, 4 * d:5 * d])
    acc = wa_ref[CONV_A_W - 1:CONV_A_W, :] * u
    for k in range(CONV_A_W - 1):
        acc = acc + wa_ref[k:k + 1, :] * ctxa_ref[k]
    mix_ref[:, 0:d] = (gate_b * acc).astype(BF16)
    accb = wb_ref[CONV_B_W - 1:CONV_B_W, :] * glu + bb_ref[...]
    for k in range(CONV_B_W - 1):
        accb = accb + wb_ref[k:k + 1, :] * ctxb_ref[k]
    mu = jnp.mean(accb, axis=-1, keepdims=True)
    cc = accb - mu
    var = jnp.mean(cc * cc, axis=-1, keepdims=True)
    y = cc * lax.rsqrt(var + LN_EPS) * ng_ref[...] + nb_ref[...]
    mix_ref[:, d:2 * d] = (y * jax.nn.sigmoid(y)).astype(BF16)
    for k in range(CONV_A_W - 2):
        sa_ref[k] = ctxa_ref[k + 1]
    sa_ref[CONV_A_W - 2] = u
    for k in range(CONV_B_W - 2):
        sb_ref[k] = ctxb_ref[k + 1]
    sb_ref[CONV_B_W - 2] = glu


def _conv_mixer_decode(proj, ctx_a_t, ctx_b_t, wa, wb, bb, ng, nb):
    s = proj.shape[0]
    d = wa.shape[1]
    return pl.pallas_call(
        functools.partial(_convmix_dec_body, d=d),
        out_shape=(jax.ShapeDtypeStruct((s, 2 * d), BF16),
                   jax.ShapeDtypeStruct((CONV_A_W - 1, s, d), F32),
                   jax.ShapeDtypeStruct((CONV_B_W - 1, s, d), F32)),
        compiler_params=pltpu.CompilerParams(vmem_limit_bytes=VMEM_LIMIT_BYTES),
    )(proj, ctx_a_t, ctx_b_t, wa, wb, bb.reshape(1, d), ng.reshape(1, d), nb.reshape(1, d))


FFN_PAD = 8


def _gelu(x):
    return 0.5 * x * (1.0 + lax.erf(x * (1.0 / math.sqrt(2.0))))


FFN_ROW_BLOCK = 512


def _ffn_up_body(x_ref, wg_ref, wv_ref, cw_ref, cb_ref, ctx_ref, h_ref, st_ref, wgk_ref, wvk_ref, wb, ext, vbuf,
                 *, tm, rb):
    t = pl.program_id(2)
    e0 = FFN_PAD - (FFN_CONV_W - 1)
    tn = wg_ref.shape[-1]

    @pl.when(jnp.logical_and(pl.program_id(1) == 0, t == 0))
    def _():
        wb[:, 0:tn] = wg_ref[...].astype(BF16)
        wb[:, tn:2 * tn] = wv_ref[...].astype(BF16)
        wgk_ref[...] = wb[:, 0:tn]
        wvk_ref[...] = wb[:, tn:2 * tn]

    @pl.when(t == 0)
    def _():
        ext[e0:FFN_PAD, :] = ctx_ref[0]

    def matmuls(r):
        gv = jnp.dot(x_ref[0, r * rb:(r + 1) * rb, :], wb[...], preferred_element_type=F32)
        ext[FFN_PAD + r * rb:FFN_PAD + (r + 1) * rb, :] = gv[:, 0:tn]
        vbuf[r * rb:(r + 1) * rb, :] = gv[:, tn:2 * tn]

    rc = min(CONV_ROW_CHUNK, rb)

    def epilogue(r):
        for r0 in range(r * rb, (r + 1) * rb, rc):
            acc = cw_ref[0:1, :] * ext[e0 + r0:e0 + r0 + rc, :] + cb_ref[...]
            for k in range(1, FFN_CONV_W):
                acc = acc + cw_ref[k:k + 1, :] * ext[e0 + r0 + k:e0 + r0 + k + rc, :]
            h_ref[0, r0:r0 + rc, :] = (_gelu(acc) * vbuf[r0:r0 + rc, :]).astype(BF16)

    nsub = tm // rb
    matmuls(0)
    for r in range(1, nsub + 1):
        epilogue(r - 1)
        if r < nsub:
            matmuls(r)

    tail = ext[e0 + tm:FFN_PAD + tm, :]
    ext[e0:FFN_PAD, :] = tail

    @pl.when(t == pl.num_programs(2) - 1)
    def _():
        st_ref[0] = tail


def _ffn_up_prompt(x, w_up, layer, cw, cb, ctx, *, tm=1024, tn=256):
    bsz, t, k = x.shape
    f = cw.shape[1]
    tm = _tile(t, tm)
    rb = _tile(tm, FFN_ROW_BLOCK)
    tn = _tile(f, tn, LANES)
    nc = f // tn
    assert rb % min(CONV_ROW_CHUNK, rb) == 0
    return pl.pallas_call(
        functools.partial(_ffn_up_body, tm=tm, rb=rb),
        grid=(nc, bsz, t // tm),
        in_specs=[pl.BlockSpec((1, tm, k), lambda c, b, i: (b, i, 0)),
                  pl.BlockSpec((None, k, tn), lambda c, b, i: (layer, 0, c)),
                  pl.BlockSpec((None, k, tn), lambda c, b, i: (layer, 0, nc + c)),
                  pl.BlockSpec((FFN_CONV_W, tn), lambda c, b, i: (0, c)),
                  pl.BlockSpec((1, tn), lambda c, b, i: (0, c)),
                  pl.BlockSpec((1, FFN_CONV_W - 1, tn), lambda c, b, i: (b, 0, c))],
        out_specs=(pl.BlockSpec((1, tm, tn), lambda c, b, i: (b, i, c)),
                   pl.BlockSpec((1, FFN_CONV_W - 1, tn), lambda c, b, i: (b, 0, c)),
                   pl.BlockSpec((k, tn), lambda c, b, i: (0, c)),
                   pl.BlockSpec((k, tn), lambda c, b, i: (0, c))),
        out_shape=(jax.ShapeDtypeStruct((bsz, t, f), BF16),
                   jax.ShapeDtypeStruct((bsz, FFN_CONV_W - 1, f), F32),
                   jax.ShapeDtypeStruct((k, f), BF16), jax.ShapeDtypeStruct((k, f), BF16)),
        scratch_shapes=[pltpu.VMEM((k, 2 * tn), BF16),
                        pltpu.VMEM((FFN_PAD + tm, tn), F32), pltpu.VMEM((tm, tn), F32)],
        compiler_params=_cparams(3),
    )(x, w_up, w_up, cw, cb.reshape(1, f), ctx)


def _ffn_up_dec_body(x_ref, wg_ref, wv_ref, cw_ref, cb_ref, ctx_ref, h_ref, st_ref):
    x = x_ref[...]
    g = jnp.dot(x, wg_ref[...], preferred_element_type=F32)
    v = jnp.dot(x, wv_ref[...], preferred_element_type=F32)
    acc = cw_ref[FFN_CONV_W - 1:FFN_CONV_W, :] * g + cb_ref[...]
    for k in range(FFN_CONV_W - 1):
        acc = acc + cw_ref[k:k + 1, :] * ctx_ref[k]
    h_ref[...] = (_gelu(acc) * v).astype(BF16)
    for k in range(FFN_CONV_W - 2):
        st_ref[k] = ctx_ref[k + 1]
    st_ref[FFN_CONV_W - 2] = g


def _ffn_up_decode(x, w_gate, w_value, cw, cb, ctx_t, *, tn=512):
    s, k = x.shape
    f = cw.shape[1]
    tn = _tile(f, tn, LANES)
    nc = f // tn
    return pl.pallas_call(
        _ffn_up_dec_body,
        grid=(nc,),
        in_specs=[pl.BlockSpec((s, k), lambda c: (0, 0)),
                  pl.BlockSpec((k, tn), lambda c: (0, c)),
                  pl.BlockSpec((k, tn), lambda c: (0, c)),
                  pl.BlockSpec((FFN_CONV_W, tn), lambda c: (0, c)),
                  pl.BlockSpec((1, tn), lambda c: (0, c)),
                  pl.BlockSpec((FFN_CONV_W - 1, s, tn), lambda c: (0, 0, c))],
        out_specs=(pl.BlockSpec((s, tn), lambda c: (0, c)),
                   pl.BlockSpec((FFN_CONV_W - 1, s, tn), lambda c: (0, 0, c))),
        out_shape=(jax.ShapeDtypeStruct((s, f), BF16),
                   jax.ShapeDtypeStruct((FFN_CONV_W - 1, s, f), F32)),
        compiler_params=_cparams(1),
    )(x, w_gate, w_value, cw, cb.reshape(1, f), ctx_t)


def _qk(q, k):
    return lax.dot_general(q, k, (((1,), (1,)), ((), ())), preferred_element_type=F32)


def _softplus(z):
    return jnp.maximum(z, 0.0) + jnp.log(1.0 + jnp.exp(-jnp.abs(z)))


def _suffix_sums(lk, tri):
    hi = lk.astype(BF16)
    lo = (lk - hi.astype(F32)).astype(BF16)
    return (jnp.dot(hi, tri, preferred_element_type=F32) + jnp.dot(lo, tri, preferred_element_type=F32))


def _strict_lower_ones(n):
    r = lax.broadcasted_iota(jnp.int32, (n, n), 0)
    c = lax.broadcasted_iota(jnp.int32, (n, n), 1)
    return jnp.where(r > c, 1.0, 0.0).astype(BF16)


def _lambda(lq1_ref, lk1_ref, lq2_ref, lk2_ref, lambda_init):
    s1 = jnp.sum(lq1_ref[...] * lk1_ref[...], axis=-1, keepdims=True)
    s2 = jnp.sum(lq2_ref[...] * lk2_ref[...], axis=-1, keepdims=True)
    return jnp.exp(s1) - jnp.exp(s2) + lambda_init


def _sb_prompt_body(q_ref, k_ref, v_ref, o_ref, acc_ref, c_ref, *, tq, hp, scale):
    qi = pl.program_id(2)
    tri = _strict_lower_ones(tq)
    row = lax.broadcasted_iota(jnp.int32, (tq, tq), 0)
    col = lax.broadcasted_iota(jnp.int32, (tq, tq), 1)

    def block(kb, diagonal):
        ks = pl.ds(pl.multiple_of(kb * tq, tq), tq)
        heads = range(hp)
        cols = [slice(hh * HD, (hh + 1) * HD) for hh in heads]
        z = [_qk(q_ref[0, :, cols[hh]], k_ref[0, ks, cols[hh]]) * scale for hh in heads]
        sp = [_softplus(z[hh]) for hh in heads]
        lk = [jnp.where(col < row, -sp[hh], 0.0) if diagonal else -sp[hh] for hh in heads]
        within = [_suffix_sums(lk[hh], tri) for hh in heads]
        block_sum = [within[hh][:, 0:1] + lk[hh][:, 0:1] for hh in heads]
        if diagonal:
            a = [jnp.where(col < row, jnp.exp(z[hh] - sp[hh] + within[hh]), 0.0) for hh in heads]
            c_new = block_sum
        else:
            c_old = [c_ref[hh] for hh in heads]
            a = [jnp.exp(z[hh] - sp[hh] + (c_old[hh] + within[hh])) for hh in heads]
            c_new = [c_old[hh] + block_sum[hh] for hh in heads]
        pv = [jnp.dot(a[hh].astype(BF16), v_ref[0, ks, cols[hh]], preferred_element_type=F32) for hh in heads]
        if not diagonal:
            pv = [acc_ref[hh] + pv[hh] for hh in heads]
        for hh in heads:
            c_ref[hh] = c_new[hh]
            acc_ref[hh] = pv[hh]

    block(qi, True)

    def body(i, carry):
        block(qi - 1 - i, False)
        return carry

    lax.fori_loop(0, qi, body, 0)
    for hh in range(hp):
        o_ref[0, :, hh * HD:(hh + 1) * HD] = acc_ref[hh].astype(o_ref.dtype)


def _sb_attention_prompt(q, k, v, *, tq=256, hp=4):
    bsz, t, d = q.shape
    n_heads = d // HD
    tq = _tile(t, tq)
    hp = _tile(n_heads, hp, 1)
    w = hp * HD
    qspec = pl.BlockSpec((1, tq, w), lambda b, h, i: (b, i, h))
    kvspec = pl.BlockSpec((1, t, w), lambda b, h, i: (b, 0, h))
    return pl.pallas_call(
        functools.partial(_sb_prompt_body, tq=tq, hp=hp, scale=HD ** -0.5),
        grid=(bsz, n_heads // hp, t // tq),
        in_specs=[qspec, kvspec, kvspec], out_specs=qspec,
        out_shape=jax.ShapeDtypeStruct((bsz, t, d), BF16),
        scratch_shapes=[pltpu.VMEM((hp, tq, HD), F32), pltpu.VMEM((hp, tq, 1), F32)],
        compiler_params=_cparams(3),
    )(q, k, v)


def _df_prompt_body(q_ref, k_ref, v_ref, lq1_ref, lk1_ref, lq2_ref, lk2_ref, g_ref, o_ref,
                    m_ref, l_ref, acc_ref, *, tq, hp, scale, lambda_init):
    qi = pl.program_id(2)
    row = lax.broadcasted_iota(jnp.int32, (tq, tq), 0)
    col = lax.broadcasted_iota(jnp.int32, (tq, tq), 1)
    w = 2 * HD

    def block(kb, diagonal):
        ks = pl.ds(pl.multiple_of(kb * tq, tq), tq)
        maps = range(2 * hp)
        cols = [slice(n * HD, (n + 1) * HD) for n in maps]
        s = [_qk(q_ref[0, :, cols[n]], k_ref[0, ks, cols[n]]) * scale for n in maps]
        if diagonal:
            s = [jnp.where(col <= row, s[n], -jnp.inf) for n in maps]
            m_new = [jnp.max(s[n], axis=-1, keepdims=True) for n in maps]
        else:
            m_old = [m_ref[n] for n in maps]
            m_new = [jnp.maximum(m_old[n], jnp.max(s[n], axis=-1, keepdims=True)) for n in maps]
            alpha = [jnp.exp(m_old[n] - m_new[n]) for n in maps]
        p = [jnp.exp(s[n] - m_new[n]) for n in maps]
        l_new = [jnp.sum(p[n], axis=-1, keepdims=True) for n in maps]
        pv = [jnp.dot(p[n].astype(BF16), v_ref[0, ks, (n // 2) * w:(n // 2 + 1) * w], preferred_element_type=F32)
              for n in maps]
        if not diagonal:
            l_new = [alpha[n] * l_ref[n] + l_new[n] for n in maps]
            pv = [alpha[n] * acc_ref[n] + pv[n] for n in maps]
        for n in maps:
            m_ref[n] = m_new[n]
            l_ref[n] = l_new[n]
            acc_ref[n] = pv[n]

    block(qi, True)

    def body(i, carry):
        block(qi - 1 - i, False)
        return carry

    lax.fori_loop(0, qi, body, 0)

    lam = _lambda(lq1_ref, lk1_ref, lq2_ref, lk2_ref, lambda_init)
    for hh in range(hp):
        o = acc_ref[2 * hh] / l_ref[2 * hh] - lam * (acc_ref[2 * hh + 1] / l_ref[2 * hh + 1])
        y = o * lax.rsqrt(jnp.mean(o * o, axis=-1, keepdims=True) + RMS_EPS) * g_ref[...]
        o_ref[0, :, hh * w:(hh + 1) * w] = (y * (1.0 - lambda_init)).astype(o_ref.dtype)


def _df_attention_prompt(q, k, v, lam_params, subln_g, lambda_init, *, tq=256, hp=4):
    bsz, t, d = q.shape
    w = 2 * HD
    n_heads = d // w
    tq = _tile(t, tq)
    hp = _tile(n_heads, hp, 1)
    wb = hp * w
    qspec = pl.BlockSpec((1, tq, wb), lambda b, h, i: (b, i, h))
    kvspec = pl.BlockSpec((1, t, wb), lambda b, h, i: (b, 0, h))
    vec = pl.BlockSpec((1, HD), lambda b, h, i: (0, 0))
    return pl.pallas_call(
        functools.partial(_df_prompt_body, tq=tq, hp=hp, scale=HD ** -0.5, lambda_init=lambda_init),
        grid=(bsz, n_heads // hp, t // tq),
        in_specs=[qspec, kvspec, kvspec, vec, vec, vec, vec, pl.BlockSpec((1, w), lambda b, h, i: (0, 0))],
        out_specs=qspec,
        out_shape=jax.ShapeDtypeStruct((bsz, t, d), BF16),
        scratch_shapes=[pltpu.VMEM((2 * hp, tq, 1), F32), pltpu.VMEM((2 * hp, tq, 1), F32),
                        pltpu.VMEM((2 * hp, tq, w), F32)],
        compiler_params=_cparams(3),
    )(q, k, v, *[p.reshape(1, HD) for p in lam_params], subln_g.reshape(1, w))


DECODE_PAGES_PER_STEP = 4


def _own_head_rows(s, n_heads, tiles, diag):
    return [jnp.sum(jnp.where(diag, s[:, j * LANES:(j + 1) * LANES], 0.0), axis=0, keepdims=True)
            for j in range(tiles)]


def _spread_rows(a, r0, n_heads, tiles, diag):
    return jnp.concatenate(
        [jnp.where(diag, jnp.broadcast_to(a[r0 + j:r0 + j + 1, :], (n_heads, LANES)), 0.0) for j in range(tiles)],
        axis=1)


def _head_diag(n_heads):
    sub = lax.broadcasted_iota(jnp.int32, (n_heads, LANES), 0)
    lane = lax.broadcasted_iota(jnp.int32, (n_heads, LANES), 1)
    return lane % n_heads == sub


def _split_dot_rhs(lhs, rhs):
    hi = rhs.astype(BF16)
    lo = (rhs - hi.astype(F32)).astype(BF16)
    return jnp.dot(lhs, hi, preferred_element_type=F32) + jnp.dot(lhs, lo, preferred_element_type=F32)


def _sb_decode_body(pt_ref, q_ref, *refs, n_heads, page, group, scale):
    k_refs = refs[:group]
    v_refs = refs[group:2 * group]
    o_ref, acc_ref, c_ref = refs[2 * group:]
    p = pl.program_id(1)
    tiles = page * n_heads // LANES
    nrow = group * tiles

    @pl.when(p == 0)
    def _():
        acc_ref[...] = jnp.zeros_like(acc_ref)
        c_ref[...] = jnp.zeros_like(c_ref)

    q = q_ref[0]
    diag = _head_diag(n_heads)
    zrows = []
    for g in range(group):
        zrows += _own_head_rows(_qk(q, k_refs[g][...].astype(BF16)), n_heads, tiles, diag)
    z = jnp.concatenate(zrows, axis=0) * scale
    sp = _softplus(z)
    lk = -sp

    li = lax.broadcasted_iota(jnp.int32, (LANES, LANES), 0)
    lj = lax.broadcasted_iota(jnp.int32, (LANES, LANES), 1)
    same_head = li % n_heads == lj % n_heads
    later_in_row = jnp.where(jnp.logical_and(same_head, li // n_heads > lj // n_heads), 1.0, 0.0).astype(BF16)
    whole_row = jnp.where(same_head, 1.0, 0.0).astype(BF16)
    within = _suffix_sums(lk, later_in_row)
    row_tot = _suffix_sums(lk, whole_row)
    ri = lax.broadcasted_iota(jnp.int32, (nrow, nrow), 0)
    rj = lax.broadcasted_iota(jnp.int32, (nrow, nrow), 1)
    later_rows = _split_dot_rhs(jnp.where(rj > ri, 1.0, 0.0).astype(BF16), row_tot)
    a = jnp.exp(z - sp + (c_ref[...] + later_rows + within))
    c_ref[...] = c_ref[...] + later_rows[0:1, :] + row_tot[0:1, :]

    upd = jnp.zeros((n_heads, HD), F32)
    for g in range(group):
        a_sp = _spread_rows(a, g * tiles, n_heads, tiles, diag).astype(BF16)
        upd = upd + jnp.dot(a_sp, v_refs[g][...].astype(BF16), preferred_element_type=F32)
    acc_ref[...] = acc_ref[...] + upd

    @pl.when(p == pl.num_programs(1) - 1)
    def _():
        o_ref[0] = acc_ref[...].astype(o_ref.dtype)


def _page_specs(block, page_table_cols, group, layer):
    def spec(g):
        return pl.BlockSpec(block, lambda b, p, pt: (pt[b, page_table_cols - (p + 1) * group + g], layer, 0, 0))
    return [spec(g) for g in range(group)]


def _sb_attention_decode(q, cache_k, cache_v, page_table, layer):
    s, n_heads, _ = q.shape
    page = cache_k.shape[2] // n_heads
    n_pages = page_table.shape[1]
    group = _tile(n_pages, DECODE_PAGES_PER_STEP, 1)
    assert (page * n_heads) % LANES == 0 and LANES % n_heads == 0
    kv = _page_specs((None, None, page * n_heads, HD), n_pages, group, layer)
    hd = pl.BlockSpec((1, n_heads, HD), lambda b, p, pt: (b, 0, 0))
    return pl.pallas_call(
        functools.partial(_sb_decode_body, n_heads=n_heads, page=page, group=group, scale=HD ** -0.5),
        grid_spec=pltpu.PrefetchScalarGridSpec(
            num_scalar_prefetch=1, grid=(s, n_pages // group), in_specs=[hd] + kv + kv, out_specs=hd,
            scratch_shapes=[pltpu.VMEM((n_heads, HD), F32), pltpu.VMEM((1, LANES), F32)]),
        out_shape=jax.ShapeDtypeStruct((s, n_heads, HD), BF16),
        compiler_params=_cparams(2),
    )(page_table, q, *([cache_k] * group), *([cache_v] * group))


def _head_lanes_max(x, n_heads):
    sh = n_heads
    while sh < LANES:
        x = jnp.maximum(x, pltpu.roll(x, sh, 1))
        sh *= 2
    return x


def _head_lanes_sum(x, n_heads):
    sh = n_heads
    while sh < LANES:
        x = x + pltpu.roll(x, sh, 1)
        sh *= 2
    return x


def _lanes_to_rows(x, n_heads, diag):
    lane = lax.broadcasted_iota(jnp.int32, (n_heads, LANES), 1)
    pick = jnp.logical_and(diag, lane < n_heads)
    return jnp.sum(jnp.where(pick, jnp.broadcast_to(x, (n_heads, LANES)), 0.0), axis=1, keepdims=True)


def _df_decode_body(pt_ref, q_ref, kn_ref, vn_ref, *refs, n_heads, page, group, scale, lambda_init):
    k_refs = refs[:group]
    v_refs = refs[group:2 * group]
    lq1_ref, lk1_ref, lq2_ref, lk2_ref, g_ref, o_ref, m_ref, l_ref, acc_ref = refs[2 * group:]
    p = pl.program_id(1)
    tiles = page * n_heads // LANES
    rows_per_map = page * n_heads

    @pl.when(p == 0)
    def _():
        m_ref[...] = jnp.full_like(m_ref, -jnp.inf)
        l_ref[...] = jnp.zeros_like(l_ref)
        acc_ref[...] = jnp.zeros_like(acc_ref)

    diag = _head_diag(n_heads)
    values = [v_refs[g][...].astype(BF16) for g in range(group)]
    srows = [[], []]
    for c in range(2):
        qc = q_ref[0, c * n_heads:(c + 1) * n_heads, :]
        for g in range(group):
            kc = k_refs[g][pl.ds(c, rows_per_map, stride=2), :].astype(BF16)
            srows[c] += _own_head_rows(_qk(qc, kc), n_heads, tiles, diag)
    s = [jnp.concatenate(srows[c], axis=0) * scale for c in range(2)]
    m_old = [m_ref[c] for c in range(2)]
    m_new = [jnp.maximum(m_old[c], _head_lanes_max(jnp.max(s[c], axis=0, keepdims=True), n_heads))
             for c in range(2)]
    alpha = [jnp.exp(m_old[c] - m_new[c]) for c in range(2)]
    pr = [jnp.exp(s[c] - m_new[c]) for c in range(2)]
    l_new = [alpha[c] * l_ref[c] + jnp.sum(pr[c], axis=0, keepdims=True) for c in range(2)]
    upd = [jnp.zeros((n_heads, 2 * HD), F32) for c in range(2)]
    for g in range(group):
        for c in range(2):
            p_sp = _spread_rows(pr[c], g * tiles, n_heads, tiles, diag).astype(BF16)
            upd[c] = upd[c] + jnp.dot(p_sp, values[g], preferred_element_type=F32)
    acc_new = [_lanes_to_rows(alpha[c], n_heads, diag) * acc_ref[c] + upd[c] for c in range(2)]
    for c in range(2):
        m_ref[c] = m_new[c]
        l_ref[c] = l_new[c]
        acc_ref[c] = acc_new[c]

    @pl.when(p == pl.num_programs(1) - 1)
    def _():
        outs = []
        for c in range(2):
            rows = slice(c * n_heads, (c + 1) * n_heads)
            m_col = _lanes_to_rows(m_ref[c], n_heads, diag)
            l_col = _lanes_to_rows(_head_lanes_sum(l_ref[c], n_heads), n_heads, diag)
            s_new = jnp.sum(q_ref[0, rows, :].astype(F32) * kn_ref[0, rows, :], axis=-1, keepdims=True) * scale
            m_fin = jnp.maximum(m_col, s_new)
            a_fin = jnp.exp(m_col - m_fin)
            p_new = jnp.exp(s_new - m_fin)
            outs.append((a_fin * acc_ref[c] + p_new * vn_ref[0]) / (a_fin * l_col + p_new))
        lam = _lambda(lq1_ref, lk1_ref, lq2_ref, lk2_ref, lambda_init)
        o = outs[0] - lam * outs[1]
        y = o * lax.rsqrt(jnp.mean(o * o, axis=-1, keepdims=True) + RMS_EPS) * g_ref[...]
        o_ref[0] = (y * (1.0 - lambda_init)).astype(o_ref.dtype)


def _df_attention_decode(q, k_new, v_new, cache_k, cache_v, page_table, layer, lam_params, subln_g, lambda_init):
    s, nr, _ = q.shape
    n_heads = nr // 2
    page = cache_v.shape[2] // n_heads
    n_pages = page_table.shape[1]
    group = _tile(n_pages, DECODE_PAGES_PER_STEP, 1)
    assert (page * n_heads) % LANES == 0 and LANES % n_heads == 0
    w = 2 * HD
    kspecs = _page_specs((None, None, page * nr, HD), n_pages, group, layer)
    vspecs = _page_specs((None, None, page * n_heads, w), n_pages, group, layer)
    qspec = pl.BlockSpec((1, nr, HD), lambda b, p, pt: (b, 0, 0))
    vnspec = pl.BlockSpec((1, n_heads, w), lambda b, p, pt: (b, 0, 0))
    vec = pl.BlockSpec((1, HD), lambda b, p, pt: (0, 0))
    return pl.pallas_call(
        functools.partial(_df_decode_body, n_heads=n_heads, page=page, group=group, scale=HD ** -0.5,
                          lambda_init=lambda_init),
        grid_spec=pltpu.PrefetchScalarGridSpec(
            num_scalar_prefetch=1, grid=(s, n_pages // group),
            in_specs=[qspec, qspec, vnspec] + kspecs + vspecs + [vec, vec, vec, vec,
                                                                 pl.BlockSpec((1, w), lambda b, p, pt: (0, 0))],
            out_specs=vnspec,
            scratch_shapes=[pltpu.VMEM((2, 1, LANES), F32), pltpu.VMEM((2, 1, LANES), F32),
                            pltpu.VMEM((2, n_heads, w), F32)]),
        out_shape=jax.ShapeDtypeStruct((s, n_heads, w), BF16),
        compiler_params=_cparams(2),
    )(page_table, q, k_new, v_new, *([cache_k] * group), *([cache_v] * group),
      *[p.reshape(1, HD) for p in lam_params], subln_g.reshape(1, w))


def _rope_tables(positions):
    half = HD // 2
    inv_freq = ROPE_THETA ** (-jnp.arange(half, dtype=F32) / half)
    ang = positions.astype(F32)[:, None] * inv_freq[None, :]
    cos, sin = jnp.cos(ang), jnp.sin(ang)
    return jnp.concatenate([cos, cos], axis=-1), jnp.concatenate([-sin, sin], axis=-1)


def kernel(x_prompt, x_sample, state_conv_a, state_conv_b, state_ffn, cache_sb_k, cache_sb_v, cache_df_k, cache_df_v, page_table, w_in_conv, conv_a_w, conv_b_w, conv_b_b, norm_b_g, norm_b_b, w_out_conv, w_in_attn, lambda_q1, lambda_k1, lambda_q2, lambda_k2, subln_g, w_out_attn, ln1_g, ln1_b, ln2_g, ln2_b, w_ffn_up, ffn_conv_w, ffn_conv_b, w_ffn_down):
    bp, t, dm = x_prompt.shape
    bs = x_sample.shape[0]
    assert x_sample.shape[1] == 1
    depth = ln1_g.shape[0]
    d_a = conv_a_w.shape[2]
    d_ff = ffn_conv_w.shape[2]
    h_sb = cache_sb_k.shape[3]
    h_df = cache_df_v.shape[3]
    d_sb = h_sb * HD
    d_df = h_df * 2 * HD
    page = cache_sb_k.shape[2]
    past_len = page_table.shape[1] * page
    alpha = (2 * depth) ** 0.25
    mp = bp * t

    hp = x_prompt.reshape(mp, dm)
    hs = x_sample.reshape(bs, dm)
    hp_b = hp.astype(BF16)
    hs_b = hs.astype(BF16)
    w_down_b = w_ffn_down.astype(BF16)

    ca_p, ca_s, cb_p, cb_s, f_p, f_s = [], [], [], [], [], []
    rows_p = ([], [], [], [])
    rows_s = ([], [], [], [])

    for layer in range(depth):
        i = layer // 2
        if layer % 2 == 0:
            w_in, w_out = w_in_conv[i], w_out_conv[i]
            cw = (conv_a_w[i], conv_b_w[i], conv_b_b[i], norm_b_g[i], norm_b_b[i])
            proj_p, w_in_b = _matmul([hp_b], w_in, (F32,), tm=PROJ_TM, tn=PROJ_TN, keep_w=True)
            mixed_p, sa, sb = _conv_mixer_prompt(
                proj_p.reshape(bp, t, 5 * d_a), jnp.zeros((bp, CONV_A_W - 1, d_a), F32),
                jnp.zeros((bp, CONV_B_W - 1, d_a), F32), *cw)
            ca_p.append(sa)
            cb_p.append(sb)
            m_p, w_out_b = _matmul([mixed_p.reshape(mp, 2 * d_a)], w_out, (F32,), tm=PROJ_TM, tn=PROJ_TN,
                                   keep_w=True)

            (proj_s,) = _matmul([hs_b], w_in_b, (F32,), tm=bs, tn=DEC_TN)
            mixed_s, sa, sb = _conv_mixer_decode(
                proj_s, jnp.swapaxes(state_conv_a[i], 0, 1), jnp.swapaxes(state_conv_b[i], 0, 1), *cw)
            ca_s.append(jnp.swapaxes(sa, 0, 1))
            cb_s.append(jnp.swapaxes(sb, 0, 1))
            (m_s,) = _matmul([mixed_s], w_out_b, (F32,), tm=bs, tn=DEC_TN)
        else:
            lambda_init = 0.8 - 0.6 * math.exp(-0.3 * layer)
            w_in, w_out = w_in_attn[i], w_out_attn[i]
            lam_params = (lambda_q1[i], lambda_k1[i], lambda_q2[i], lambda_k2[i])
            edges = (0, d_sb, 2 * d_sb, 3 * d_sb, 3 * d_sb + d_df, 3 * d_sb + 2 * d_df, 3 * d_sb + 3 * d_df)
            span = lambda n: (edges[n], edges[n + 1])

            def qkv_proj(x_b, tm, tn, rope, weights):
                outs = []
                for n in range(6):
                    dts = (BF16,) if n in (0, 3) else (F32, BF16)
                    r = rope if n in (3, 4) else None
                    if isinstance(weights, tuple):
                        outs.append(_matmul([x_b], weights[n], dts, tm=tm, tn=tn, rope=r))
                    else:
                        outs.append(_matmul([x_b], weights, dts, tm=tm, tn=tn, rope=r, cols=span(n), keep_w=True))
                return outs

            cos_p, sin_p = _rope_tables(jnp.arange(t, dtype=jnp.int32))
            groups = qkv_proj(hp_b, PROJ_TM, PROJ_TN, (cos_p, sin_p, t), w_in)
            w_in_b = tuple(g[-1] for g in groups)
            (q_sb,), k_sb, v_sb, (q_df,), k_df, v_df = [g[:-1] for g in groups]
            new_p = (k_sb[0].reshape(bp, t, h_sb, HD), v_sb[0].reshape(bp, t, h_sb, HD),
                     k_df[0].reshape(bp, t, h_df, 2, HD), v_df[0].reshape(bp, t, h_df, 2 * HD))
            b3 = lambda a: a.reshape(bp, t, -1)
            o_sb = _sb_attention_prompt(b3(q_sb), b3(k_sb[1]), b3(v_sb[1]))
            o_df = _df_attention_prompt(b3(q_df), b3(k_df[1]), b3(v_df[1]), lam_params, subln_g[i], lambda_init)
            m_p, w_out_b = _matmul([o_sb.reshape(mp, d_sb), o_df.reshape(mp, d_df)], w_out, (F32,),
                                   tm=PROJ_TM, tn=PROJ_TN, keep_w=True)

            cos_s, sin_s = _rope_tables(jnp.full((1,), past_len, jnp.int32))
            (q_sb,), k_sb, v_sb, (q_df,), k_df, v_df = qkv_proj(hs_b, bs, DEC_TN, (cos_s, sin_s, 1), w_in_b)
            new_s = (k_sb[0].reshape(bs, 1, h_sb, HD), v_sb[0].reshape(bs, 1, h_sb, HD),
                     k_df[0].reshape(bs, 1, h_df, 2, HD), v_df[0].reshape(bs, 1, h_df, 2 * HD))
            pool, n_layers = cache_sb_k.shape[:2]
            o_sb_s = _sb_attention_decode(
                q_sb.reshape(bs, h_sb, HD), cache_sb_k.reshape(pool, n_layers, page * h_sb, HD),
                cache_sb_v.reshape(pool, n_layers, page * h_sb, HD), page_table, i)
            q_dec = jnp.swapaxes(q_df.reshape(bs, h_df, 2, HD), 1, 2).reshape(bs, 2 * h_df, HD)
            k_dec = jnp.swapaxes(k_df[0].reshape(bs, h_df, 2, HD), 1, 2).reshape(bs, 2 * h_df, HD)
            o_df_s = _df_attention_decode(
                q_dec, k_dec, v_df[0].reshape(bs, h_df, 2 * HD),
                cache_df_k.reshape(pool, n_layers, page * h_df * 2, HD),
                cache_df_v.reshape(pool, n_layers, page * h_df, 2 * HD), page_table, i,
                lam_params, subln_g[i], lambda_init)
            (m_s,) = _matmul([o_sb_s.reshape(bs, d_sb), o_df_s.reshape(bs, d_df)], w_out_b, (F32,),
                             tm=bs, tn=DEC_TN)
            for lst, r in zip(rows_p, new_p):
                lst.append(r)
            for lst, r in zip(rows_s, new_s):
                lst.append(r)

        hp, hp_b = _deepnorm_ln(hp, m_p, ln1_g[layer], ln1_b[layer], alpha)
        hs, hs_b = _deepnorm_ln(hs, m_s, ln1_g[layer], ln1_b[layer], alpha)

        h_p, st_p, w_gate_b, w_value_b = _ffn_up_prompt(
            hp_b.reshape(bp, t, dm), w_ffn_up, layer, ffn_conv_w[layer], ffn_conv_b[layer],
            jnp.zeros((bp, FFN_CONV_W - 1, d_ff), F32))
        f_p.append(st_p)
        (fp,) = _matmul([h_p.reshape(mp, d_ff)], w_down_b, (F32,), tm=512, tn=512, layer=layer)
        h_s, st_s = _ffn_up_decode(hs_b, w_gate_b, w_value_b, ffn_conv_w[layer], ffn_conv_b[layer],
                                   jnp.swapaxes(state_ffn[layer], 0, 1))
        f_s.append(jnp.swapaxes(st_s, 0, 1))
        (fs,) = _matmul([h_s], w_down_b, (F32,), tm=bs, tn=512, layer=layer)

        hp, hp_b = _deepnorm_ln(hp, fp, ln2_g[layer], ln2_b[layer], alpha)
        hs, hs_b = _deepnorm_ln(hs, fs, ln2_g[layer], ln2_b[layer], alpha)

    return (hp.reshape(bp, t, dm), hs.reshape(bs, 1, dm),
            jnp.stack(ca_p), jnp.stack(ca_s), jnp.stack(cb_p), jnp.stack(cb_s),
            jnp.stack(f_p), jnp.stack(f_s),
            jnp.stack(rows_p[0], axis=1), jnp.stack(rows_s[0], axis=1),
            jnp.stack(rows_p[1], axis=1), jnp.stack(rows_s[1], axis=1),
            jnp.stack(rows_p[2], axis=1), jnp.stack(rows_s[2], axis=1),
            jnp.stack(rows_p[3], axis=1), jnp.stack(rows_s[3], axis=1))
```
